```python
import math
import jax
import jax.numpy as jnp
from jax import lax
import numpy as np

D_MODEL = 1024
BATCH = 4
SEQ = 4096
DEPTH = 2

CTX_LEN = 256
GRID_W = 64
CONV_W = 3
NORM_EPS = 1e-6
F32 = jnp.float32

M_HEADS = 4
M_HEAD_DIM = 64
M_INNER = M_HEADS * M_HEAD_DIM
M_GROUPS = 2
M_STATE = 128
M_CONV_DIM = M_INNER + 2 * M_GROUPS * M_STATE
M_COLS = M_INNER + M_CONV_DIM + 2 * M_HEADS
M_CHUNK = 128

DN_HEADS = 4
DN_HEAD_K = 128
DN_HEAD_V = 128
DN_KD = DN_HEADS * DN_HEAD_K
DN_VD = DN_HEADS * DN_HEAD_V
DN_CONV_DIM = 2 * DN_KD + DN_VD
DN_COLS = DN_CONV_DIM + DN_VD + 4 * DN_HEADS
DN_CHUNK = 64

RW_HEADS = 4
RW_HEAD = 64
RW_DIM = RW_HEADS * RW_HEAD
RW_DECAY_LORA = 64
RW_ICLR_LORA = 64
RW_GATE_LORA = 128
RW_COLS = 3 * RW_DIM + 2 * RW_DECAY_LORA + RW_ICLR_LORA + RW_GATE_LORA
RW_LN_EPS = RW_HEAD * 1e-5

IN_COLS = M_COLS + DN_COLS + RW_COLS
MIX_OUT = M_INNER + DN_VD + RW_DIM

N_EXPERTS = 32
TOP_K = 4
D_FF = 1024
SWIGLU_ALPHA = 1.702
SWIGLU_LIMIT = 7.0
MOE_BLOCK = 256

kernel_name = 'hybrid_ssd_deltanet_rwkv7_moe_prefix_dit'


def split_cols(t, sizes):
    return jnp.split(t, [int(o) for o in np.cumsum(sizes)[:-1]], axis=-1)


def rms_norm(t, w, eps=NORM_EPS):
    tf = t.astype(F32)
    y = tf * lax.rsqrt(jnp.mean(tf * tf, axis=-1, keepdims=True) + eps)
    return (y * w).astype(t.dtype)


def l2norm(t, eps=1e-6):
    return t * lax.rsqrt(jnp.sum(t * t, axis=-1, keepdims=True) + eps)


def modulate(h, shift, scale):
    return h * (1 + scale) + shift


def centred_conv(t, w):
    half = CONV_W // 2
    n = t.shape[1]
    tp = jnp.pad(t, ((0, 0), (half, half), (0, 0)))
    out = tp[:, 0:n] * w[0]
    for j in range(1, CONV_W):
        out = out + tp[:, j:j + n] * w[j]
    return out


def to_col_major(t, rows):
    b, n, ch = t.shape
    return t.reshape(b, rows, GRID_W, ch).transpose(0, 2, 1, 3).reshape(b, n, ch)


def to_raster(t, rows):
    b, n, ch = t.shape
    return t.reshape(b, GRID_W, rows, ch).transpose(0, 2, 1, 3).reshape(b, n, ch)


def q_shift(t, rows):
    b, n, ch = t.shape
    g = t.reshape(b, rows, GRID_W, ch)
    from_left = jnp.pad(g[:, :, :-1], ((0, 0), (0, 0), (1, 0), (0, 0)))
    from_right = jnp.pad(g[:, :, 1:], ((0, 0), (0, 0), (0, 1), (0, 0)))
    from_up = jnp.pad(g[:, :-1], ((0, 0), (1, 0), (0, 0), (0, 0)))
    from_down = jnp.pad(g[:, 1:], ((0, 0), (0, 1), (0, 0), (0, 0)))
    sel = jnp.arange(ch) % 4
    out = jnp.where(sel == 0, from_left, jnp.where(sel == 1, from_right, jnp.where(sel == 2, from_up, from_down)))
    return out.reshape(b, n, ch)


def bi_shift(t):
    prev = jnp.pad(t[:, :-1], ((0, 0), (1, 0), (0, 0)))
    nxt = jnp.pad(t[:, 1:], ((0, 0), (0, 1), (0, 0)))
    return jnp.where(jnp.arange(t.shape[-1]) % 2 == 0, prev, nxt)


def two_segment_scan(scan_fn, ctx_args, lat_args, state0, reverse):
    flip = (lambda t: jnp.flip(t, axis=1)) if reverse else (lambda t: t)
    y_c, s_c = scan_fn(*[flip(t) for t in ctx_args], state0)
    y_l, _ = scan_fn(*[flip(t) for t in lat_args], s_c)
    return flip(y_c), flip(y_l)


def ssd_chunk_scan(xs, log_a, bm, cm, h0):
    b, n, H, P = xs.shape
    nc = n // M_CHUNK
    ch = lambda t: t.reshape((b, nc, M_CHUNK) + t.shape[2:])
    xs, log_a, bm, cm = ch(xs), ch(log_a), ch(bm), ch(cm)
    a_cum = jnp.cumsum(log_a, axis=2)
    idx = jnp.arange(M_CHUNK)
    incl = (idx[:, None] >= idx[None, :])[None, None, :, :, None]
    seg = jnp.exp(jnp.where(incl, a_cum[:, :, :, None, :] - a_cum[:, :, None, :, :], -jnp.inf))
    scores = jnp.einsum('bcihn,bcjhn->bcijh', cm, bm) * seg
    y_diag = jnp.einsum('bcijh,bcjhp->bcihp', scores, xs)
    to_end = jnp.exp(a_cum[:, :, -1:] - a_cum)
    states = jnp.einsum('bcjhn,bcjhp->bchpn', bm * to_end[..., None], xs)
    chunk_decay = jnp.exp(a_cum[:, :, -1])

    def step(h, inp):
        s, dec = inp
        return h * dec[:, :, None, None] + s, h

    h_last, h_in = lax.scan(step, h0, (jnp.moveaxis(states, 1, 0), jnp.moveaxis(chunk_decay, 1, 0)))
    h_in = jnp.moveaxis(h_in, 0, 1)
    y_off = jnp.einsum('bcihn,bchpn->bcihp', cm, h_in) * jnp.exp(a_cum)[..., None]
    return (y_diag + y_off).reshape(b, n, H, P), h_last


def gated_delta_chunk_scan(q, k, v, g, beta, s0):
    b, n, H, K = q.shape
    V = v.shape[-1]
    nc = n // DN_CHUNK
    ch = lambda t: t.reshape((b, nc, DN_CHUNK) + t.shape[2:])
    q, k, v, g, beta = ch(q), ch(k), ch(v), ch(g), ch(beta)
    g_cum = jnp.cumsum(g, axis=2)
    idx = jnp.arange(DN_CHUNK)
    incl = idx[:, None] >= idx[None, :]
    strict = idx[:, None] > idx[None, :]
    gh = jnp.moveaxis(g_cum, 3, 2)
    decay = jnp.exp(jnp.where(incl, gh[..., :, None] - gh[..., None, :], -jnp.inf))
    kb = k * beta[..., None]
    lower = jnp.where(strict, jnp.einsum('bcihk,bcjhk->bchij', kb, k) * decay, 0.0)
    eye = jnp.eye(DN_CHUNK, dtype=q.dtype)
    rhs = jnp.concatenate([v * beta[..., None], kb * jnp.exp(g_cum)[..., None]], axis=-1)
    rhs = jnp.moveaxis(rhs, 3, 2)
    sol = lax.linalg.triangular_solve(eye + lower, rhs, left_side=True, lower=True, unit_diagonal=True)
    u, w = sol[..., :V], sol[..., V:]
    attn = jnp.einsum('bcihk,bcjhk->bchij', q, k) * decay
    q_dec = jnp.moveaxis(q * jnp.exp(g_cum)[..., None], 3, 2)
    k_dec = jnp.moveaxis(k * jnp.exp(g_cum[:, :, -1:] - g_cum)[..., None], 3, 2)
    g_last = jnp.exp(g_cum[:, :, -1])

    def step(S, inp):
        u_i, w_i, attn_i, qd_i, kd_i, gl_i = inp
        v_new = u_i - jnp.einsum('bhck,bhkv->bhcv', w_i, S)
        o = jnp.einsum('bhck,bhkv->bhcv', qd_i, S) + jnp.einsum('bhij,bhjv->bhiv', attn_i, v_new)
        S = S * gl_i[..., None, None] + jnp.einsum('bhck,bhcv->bhkv', kd_i, v_new)
        return S, o

    xs = tuple(jnp.moveaxis(t, 1, 0) for t in (u, w, attn, q_dec, k_dec, g_last))
    s_last, o = lax.scan(step, s0, xs)
    o = jnp.transpose(o, (1, 0, 3, 2, 4)).reshape(b, n, H, V)
    return o, s_last


def rwkv7_scan(r, log_w, k, v, a_vec, b_vec, s0):
    def step(S, inp):
        r_t, w_t, k_t, v_t, a_t, b_t = inp
        sa = jnp.einsum('bhvk,bhk->bhv', S, a_t)
        S = S * w_t[:, :, None, :] + sa[..., None] * b_t[:, :, None, :] + v_t[..., None] * k_t[:, :, None, :]
        return S, jnp.einsum('bhvk,bhk->bhv', S, r_t)

    xs = tuple(jnp.moveaxis(t, 1, 0) for t in (r, jnp.exp(log_w), k, v, a_vec, b_vec))
    s_last, y = lax.scan(step, s0, xs)
    return jnp.moveaxis(y, 0, 1), s_last


def mamba2_mixer(p_c, p_l, conv_w, conv_b, dt_bias, a_log, d_skip, norm_w):
    a_neg = -jnp.exp(a_log.astype(F32))
    rep = M_HEADS // M_GROUPS

    def prep(p):
        b, n = p.shape[:2]
        z, xbc, dt_raw = split_cols(p, (M_INNER, M_CONV_DIM, 2 * M_HEADS))
        xbc = jax.nn.silu(centred_conv(xbc, conv_w) + conv_b)
        xs, bm, cm = split_cols(xbc, (M_INNER, M_GROUPS * M_STATE, M_GROUPS * M_STATE))
        grp = lambda t: jnp.repeat(t.astype(F32).reshape(b, n, M_GROUPS, M_STATE), rep, axis=2)
        return z, xs.astype(F32).reshape(b, n, M_HEADS, M_HEAD_DIM), grp(bm), grp(cm), dt_raw.astype(F32)

    def scan_args(direction, xs, bm, cm, dt_raw):
        dt = jax.nn.softplus(dt_raw[..., direction * M_HEADS:(direction + 1) * M_HEADS] + dt_bias[direction])
        return (xs * dt[..., None], dt * a_neg[direction], bm, cm)

    zc, xc, bc, cc, dtc = prep(p_c)
    zl, xl, bl, cl, dtl = prep(p_l)
    state0 = jnp.zeros((p_c.shape[0], M_HEADS, M_HEAD_DIM, M_STATE), F32)
    yc = xc * d_skip[:, None]
    yl = xl * d_skip[:, None]
    for direction in range(2):
        oc, ol = two_segment_scan(ssd_chunk_scan, scan_args(direction, xc, bc, cc, dtc),
                                  scan_args(direction, xl, bl, cl, dtl), state0, direction == 1)
        yc, yl = yc + oc, yl + ol

    def finish(y, z):
        b, n = z.shape[:2]
        y = y.reshape(b, n, M_INNER) * jax.nn.silu(z.astype(F32))
        y = y.reshape(b, n, M_GROUPS, M_INNER // M_GROUPS)
        y = y * lax.rsqrt(jnp.mean(y * y, axis=-1, keepdims=True) + 1e-5)
        return (y.reshape(b, n, M_INNER) * norm_w).astype(z.dtype)

    return finish(yc, zc), finish(yl, zl)


def gated_deltanet_mixer(p_c, p_l, rows, conv_w, dt_bias, a_log, norm_w):
    a_pos = jnp.exp(a_log.astype(F32))

    def prep(p):
        b, n = p.shape[:2]
        qkv, gate, beta_raw, a_raw = split_cols(p, (DN_CONV_DIM, DN_VD, 2 * DN_HEADS, 2 * DN_HEADS))
        qkv = jax.nn.silu(centred_conv(qkv, conv_w)).astype(F32)
        q, k, v = split_cols(qkv, (DN_KD, DN_KD, DN_VD))
        q = l2norm(q.reshape(b, n, DN_HEADS, DN_HEAD_K)) * DN_HEAD_K ** -0.5
        k = l2norm(k.reshape(b, n, DN_HEADS, DN_HEAD_K))
        v = v.reshape(b, n, DN_HEADS, DN_HEAD_V)
        return q, k, v, gate, beta_raw.astype(F32), a_raw.astype(F32)

    def scan_args(direction, q, k, v, beta_raw, a_raw):
        sl = slice(direction * DN_HEADS, (direction + 1) * DN_HEADS)
        beta = jax.nn.sigmoid(beta_raw[..., sl])
        g = -a_pos[direction] * jax.nn.softplus(a_raw[..., sl] + dt_bias[direction])
        return (q, k, v, g, beta)

    qc, kc, vc, gate_c, brc, arc = prep(p_c)
    ql, kl, vl, gate_l, brl, arl = prep(to_col_major(p_l, rows))
    state0 = jnp.zeros((p_c.shape[0], DN_HEADS, DN_HEAD_K, DN_HEAD_V), F32)
    oc, ol = 0.0, 0.0
    for direction in range(2):
        dc, dl = two_segment_scan(gated_delta_chunk_scan, scan_args(direction, qc, kc, vc, brc, arc),
                                  scan_args(direction, ql, kl, vl, brl, arl), state0, direction == 1)
        oc, ol = oc + dc, ol + dl

    def finish(o, gate):
        b, n = gate.shape[:2]
        o = o * lax.rsqrt(jnp.mean(o * o, axis=-1, keepdims=True) + 1e-6) * norm_w
        return (o.reshape(b, n, DN_VD) * jax.nn.silu(gate.astype(F32))).astype(gate.dtype)

    return finish(oc, gate_c), to_raster(finish(ol, gate_l), rows)


def rwkv7_mixer(p_c, p_l, rows, mu, w0, w2, a0, a2, g2, k_k, k_a, r_k, ln_w, ln_b):
    def prep(p, shifted):
        b, n = p.shape[:2]
        p = p + (shifted - p) * mu
        r, k, v, wd, ad, gd = split_cols(p, (RW_DIM, RW_DIM, RW_DIM, 2 * RW_DECAY_LORA, RW_ICLR_LORA, RW_GATE_LORA))
        heads = lambda t: t.astype(F32).reshape(b, n, RW_HEADS, RW_HEAD)
        a = jax.nn.sigmoid(a0 + ad @ a2)
        g = jax.nn.sigmoid(gd) @ g2
        kk = l2norm(heads(k * k_k))
        k = k * (1 + (a - 1) * k_a)
        log_w = [heads(-jnp.exp(-jax.nn.softplus(-(w0[j] + jnp.tanh(wd_j) @ w2[j]).astype(F32)) - 0.5))
                 for j, wd_j in enumerate(jnp.split(wd, 2, axis=-1))]
        return heads(r), heads(k), heads(v), kk, heads(a), g, log_w

    rc, kc, vc, kkc, ac, gc, lwc = prep(p_c, bi_shift(p_c))
    rl, kl, vl, kkl, al, gl, lwl = prep(p_l, q_shift(p_l, rows))
    state0 = jnp.zeros((p_c.shape[0], RW_HEADS, RW_HEAD, RW_HEAD), F32)
    yc, yl = 0.0, 0.0
    for j in range(2):
        oc, ol = two_segment_scan(rwkv7_scan, (rc, lwc[j], kc, vc, -kkc, kkc * ac),
                                  (rl, lwl[j], kl, vl, -kkl, kkl * al), state0, j == 1)
        yc, yl = yc + oc, yl + ol

    def finish(y, r, k, v, g):
        b, n = y.shape[:2]
        mean = jnp.mean(y, axis=-1, keepdims=True)
        var = jnp.mean(jnp.square(y - mean), axis=-1, keepdims=True)
        y = ((y - mean) * lax.rsqrt(var + RW_LN_EPS)).reshape(b, n, RW_DIM) * ln_w + ln_b
        bonus = (jnp.sum(r * k * r_k, axis=-1, keepdims=True) * v).reshape(b, n, RW_DIM)
        return ((y + bonus) * g).astype(p_c.dtype)

    return finish(yc, rc, kc, vc, gc), finish(yl, rl, kl, vl, gl)


def moe_ffn(h, w_router, b_router, w_gate_up, b_gate_up, w_down, b_down):
    n, d = h.shape
    logits = (h @ w_router).astype(F32) + b_router
    top_logit, top_e = lax.top_k(logits, TOP_K)
    gate = jax.nn.softmax(top_logit, axis=-1)
    nk = n * TOP_K
    flat_e = top_e.reshape(nk)
    flat_tok = jnp.arange(nk, dtype=jnp.int32) // TOP_K
    order = jnp.argsort(flat_e)
    sorted_e = flat_e[order]
    counts = jnp.bincount(flat_e, length=N_EXPERTS)
    padded = (counts + MOE_BLOCK - 1) // MOE_BLOCK * MOE_BLOCK
    pad_start = jnp.cumsum(padded) - padded
    start = jnp.cumsum(counts) - counts
    dest = pad_start[sorted_e] + jnp.arange(nk) - start[sorted_e]
    n_blocks = -(-nk // MOE_BLOCK) + N_EXPERTS
    slots = n_blocks * MOE_BLOCK
    slot_tok = jnp.full((slots,), n, jnp.int32).at[dest].set(flat_tok[order])
    slot_gate = jnp.zeros((slots,), h.dtype).at[dest].set(gate.reshape(nk)[order].astype(h.dtype))
    block_end = jnp.cumsum(padded) // MOE_BLOCK
    block_expert = jnp.minimum(jnp.searchsorted(block_end, jnp.arange(n_blocks), side='right'), N_EXPERTS - 1)
    h_pad = jnp.concatenate([h, jnp.zeros((1, d), h.dtype)], axis=0)

    def expert_block(args):
        tok, e = args
        gu = h_pad[tok] @ w_gate_up[e] + b_gate_up[e]
        g_, u_ = jnp.split(gu, 2, axis=-1)
        g_ = jnp.minimum(g_, SWIGLU_LIMIT)
        u_ = jnp.clip(u_, -SWIGLU_LIMIT, SWIGLU_LIMIT)
        act = g_ * jax.nn.sigmoid(SWIGLU_ALPHA * g_) * (u_ + 1)
        return act @ w_down[e] + b_down[e]

    y_slots = lax.map(expert_block, (slot_tok.reshape(n_blocks, MOE_BLOCK), block_expert))
    out = jnp.zeros((n + 1, d), h.dtype).at[slot_tok].add(y_slots.reshape(slots, d) * slot_gate[:, None])
    return out[:n]


def setup_inputs(seed: int = 0) -> dict:
    key = jax.random.key(seed)
    ks = iter(jax.random.split(key, 48))
    nrm = lambda shape, scale: scale * jax.random.normal(next(ks), shape, F32)
    unif = lambda shape, lo, hi: jax.random.uniform(next(ks), shape, F32, lo, hi)
    gain = lambda shape: 1.0 + nrm(shape, 0.01)

    def dt_bias(shape):
        dt = jnp.exp(unif(shape, math.log(1e-3), math.log(1e-1)))
        return dt + jnp.log(-jnp.expm1(-dt))

    L, D = DEPTH, D_MODEL
    return {
        'x': nrm((BATCH, SEQ, D), 1.0),
        'c': nrm((BATCH, D), 1.0),
        'ctx': nrm((BATCH, CTX_LEN, D), 1.0),
        'c_ctx': nrm((D,), 1.0),
        'w_mod': nrm((L, D, 6 * D), 0.5 * D ** -0.5),
        'b_mod': nrm((L, 6 * D), 0.01),
        'norm1_w': gain((L, D)),
        'norm2_w': gain((L, D)),
        'w_in': nrm((L, D, IN_COLS), D ** -0.5),
        'w_out': nrm((L, MIX_OUT, D), MIX_OUT ** -0.5),
        'm_conv_w': nrm((L, CONV_W, M_CONV_DIM), CONV_W ** -0.5),
        'm_conv_b': nrm((L, M_CONV_DIM), 0.01),
        'm_dt_bias': dt_bias((L, 2, M_HEADS)),
        'm_a_log': jnp.log(unif((L, 2, M_HEADS), 1.0, 16.0)),
        'm_d': 1.0 + nrm((L, M_HEADS), 0.1),
        'm_norm_w': gain((L, M_INNER)),
        'dn_conv_w': nrm((L, CONV_W, DN_CONV_DIM), CONV_W ** -0.5),
        'dn_dt_bias': dt_bias((L, 2, DN_HEADS)),
        'dn_a_log': jnp.log(unif((L, 2, DN_HEADS), 1.0, 16.0)),
        'dn_norm_w': gain((L, DN_HEAD_V)),
        'rw_mu': unif((L, RW_COLS), 0.0, 1.0),
        'rw_w0': unif((L, 2, RW_DIM), -5.0, 0.0),
        'rw_w2': nrm((L, 2, RW_DECAY_LORA, RW_DIM), 0.1 * RW_DECAY_LORA ** -0.5),
        'rw_a0': nrm((L, RW_DIM), 0.1),
        'rw_a2': nrm((L, RW_ICLR_LORA, RW_DIM), 0.1 * RW_ICLR_LORA ** -0.5),
        'rw_g2': nrm((L, RW_GATE_LORA, RW_DIM), RW_GATE_LORA ** -0.5),
        'rw_k_k': 0.85 + nrm((L, RW_DIM), 0.05),
        'rw_k_a': 1.0 + nrm((L, RW_DIM), 0.05),
        'rw_r_k': nrm((L, RW_HEADS, RW_HEAD), 0.1),
        'rw_ln_w': gain((L, RW_DIM)),
        'rw_ln_b': nrm((L, RW_DIM), 0.01),
        'w_router': nrm((L, D, N_EXPERTS), D ** -0.5),
        'b_router': nrm((L, N_EXPERTS), 0.01),
        'w_gate_up': nrm((L, N_EXPERTS, D, 2 * D_FF), D ** -0.5),
        'b_gate_up': nrm((L, N_EXPERTS, 2 * D_FF), 0.01),
        'w_down': nrm((L, N_EXPERTS, D_FF, D), D_FF ** -0.5),
        'b_down': nrm((L, N_EXPERTS, D), 0.01),
        'norm_f_w': gain((D,)),
    }


def reference(x, c, ctx, c_ctx, w_mod, b_mod, norm1_w, norm2_w, w_in, w_out,
              m_conv_w, m_conv_b, m_dt_bias, m_a_log, m_d, m_norm_w,
              dn_conv_w, dn_dt_bias, dn_a_log, dn_norm_w,
              rw_mu, rw_w0, rw_w2, rw_a0, rw_a2, rw_g2, rw_k_k, rw_k_a, rw_r_k, rw_ln_w, rw_ln_b,
              w_router, b_router, w_gate_up, b_gate_up, w_down, b_down, norm_f_w):
    seq, d = x.shape[1], x.shape[2]
    rows = seq // GRID_W
    x_l, x_c = x, ctx
    silu_c = jax.nn.silu(c)
    silu_cc = jax.nn.silu(c_ctx)
    for i in range(DEPTH):
        last = i == DEPTH - 1
        mod_l = jnp.split((silu_c @ w_mod[i] + b_mod[i])[:, None, :], 6, axis=-1)
        mod_c = jnp.split(silu_cc @ w_mod[i] + b_mod[i], 6, axis=-1)
        h_l = modulate(rms_norm(x_l, norm1_w[i]), mod_l[0], mod_l[1])
        h_c = modulate(rms_norm(x_c, norm1_w[i]), mod_c[0], mod_c[1])
        pa_l, pb_l, pc_l = split_cols(h_l @ w_in[i], (M_COLS, DN_COLS, RW_COLS))
        pa_c, pb_c, pc_c = split_cols(h_c @ w_in[i], (M_COLS, DN_COLS, RW_COLS))
        ya_c, ya_l = mamba2_mixer(pa_c, pa_l, m_conv_w[i], m_conv_b[i], m_dt_bias[i], m_a_log[i], m_d[i], m_norm_w[i])
        yb_c, yb_l = gated_deltanet_mixer(pb_c, pb_l, rows, dn_conv_w[i], dn_dt_bias[i], dn_a_log[i], dn_norm_w[i])
        yc_c, yc_l = rwkv7_mixer(pc_c, pc_l, rows, rw_mu[i], rw_w0[i], rw_w2[i], rw_a0[i], rw_a2[i], rw_g2[i],
                                 rw_k_k[i], rw_k_a[i], rw_r_k[i], rw_ln_w[i], rw_ln_b[i])
        x_l = x_l + mod_l[2] * (jnp.concatenate([ya_l, yb_l, yc_l], axis=-1) @ w_out[i])
        h2_l = modulate(rms_norm(x_l, norm2_w[i]), mod_l[3], mod_l[4])
        moe_args = (w_router[i], b_router[i], w_gate_up[i], b_gate_up[i], w_down[i], b_down[i])
        if last:
            x_l = x_l + mod_l[5] * moe_ffn(h2_l.reshape(-1, d), *moe_args).reshape(x_l.shape)
        else:
            x_c = x_c + mod_c[2] * (jnp.concatenate([ya_c, yb_c, yc_c], axis=-1) @ w_out[i])
            h2_c = modulate(rms_norm(x_c, norm2_w[i]), mod_c[3], mod_c[4])
            n_c = h2_c.shape[0] * h2_c.shape[1]
            f = moe_ffn(jnp.concatenate([h2_c.reshape(-1, d), h2_l.reshape(-1, d)], axis=0), *moe_args)
            x_c = x_c + mod_c[5] * f[:n_c].reshape(x_c.shape)
            x_l = x_l + mod_l[5] * f[n_c:].reshape(x_l.shape)
    return rms_norm(x_l, norm_f_w)
```

```python
import functools
import math

import jax
import jax.numpy as jnp
from jax import lax
from jax.experimental import pallas as pl
from jax.experimental.pallas import tpu as pltpu

F32 = jnp.float32
BF16 = jnp.bfloat16

D_MODEL = 1024
GRID_W = 64
NORM_EPS = 1e-6

M_HEADS, M_HEAD_DIM, M_STATE, M_GROUPS = 4, 64, 128, 2
M_INNER = M_HEADS * M_HEAD_DIM
M_CONV_DIM = M_INNER + 2 * M_GROUPS * M_STATE
M_COLS = M_INNER + M_CONV_DIM + 2 * M_HEADS
M_CHUNK = 128

DN_HEADS, DN_HEAD_K, DN_HEAD_V = 4, 128, 128
DN_KD = DN_HEADS * DN_HEAD_K
DN_VD = DN_HEADS * DN_HEAD_V
DN_CONV_DIM = 2 * DN_KD + DN_VD
DN_COLS = DN_CONV_DIM + DN_VD + 4 * DN_HEADS
DN_CHUNK = 64

RW_HEADS, RW_HEAD = 4, 64
RW_DIM = RW_HEADS * RW_HEAD
RW_DECAY_LORA, RW_ICLR_LORA, RW_GATE_LORA = 64, 64, 128
RW_COLS = 3 * RW_DIM + 2 * RW_DECAY_LORA + RW_ICLR_LORA + RW_GATE_LORA
RW_COLS_PAD = 1152
RW_LN_EPS = RW_HEAD * 1e-5
RW_CHUNK = 64
RW_NOUT = 9

N_EXPERTS, TOP_K = 32, 4
D_FF = 1024
SWIGLU_ALPHA, SWIGLU_LIMIT = 1.702, 7.0
MOE_BLOCK = 256

LANES = 128
VMEM_LIMIT = 56 * 1024 * 1024

IN_TM = 512
CONV_TM = 256
DN_G = 8
RW_TM = 256
OUT_TM = 256
MOE_TM = 256


def _cparams(n_axes):
    return pltpu.CompilerParams(dimension_semantics=("arbitrary",) * n_axes, vmem_limit_bytes=VMEM_LIMIT)


def _sigmoid(x):
    return 1.0 / (1.0 + jnp.exp(-x))


def _silu(x):
    return x * _sigmoid(x)


def _softplus(x):
    return jnp.maximum(x, 0.0) + jnp.log(1.0 + jnp.exp(-jnp.abs(x)))


def _mm(a, b):
    return jnp.dot(a.astype(BF16), b.astype(BF16), preferred_element_type=F32)


def _mm_nt(a, b):
    return lax.dot_general(a.astype(BF16), b.astype(BF16), (((1,), (1,)), ((), ())), preferred_element_type=F32)


def _mm_tn(a, b):
    return lax.dot_general(a.astype(BF16), b.astype(BF16), (((0,), (0,)), ((), ())), preferred_element_type=F32)


def _split3(x):
    hi = x.astype(BF16)
    r1 = x - hi.astype(F32)
    mid = r1.astype(BF16)
    lo = (r1 - mid.astype(F32)).astype(BF16)
    return hi, mid, lo


def _mm3(a, b):
    ah, am, _ = _split3(a)
    bh, bm, _ = _split3(b)
    dot = functools.partial(jnp.dot, preferred_element_type=F32)
    return dot(ah, bh) + (dot(am, bh) + dot(ah, bm))


def _mm_exact_lhs(a01, x):
    a = a01.astype(BF16)
    dot = functools.partial(jnp.dot, preferred_element_type=F32)
    hi, mid, lo = _split3(x)
    return dot(a, hi) + (dot(a, mid) + dot(a, lo))


def _mm_exact_rhs(x, b01):
    b = b01.astype(BF16)
    dot = functools.partial(jnp.dot, preferred_element_type=F32)
    hi, mid, lo = _split3(x)
    return dot(hi, b) + (dot(mid, b) + dot(lo, b))


def _to_rows(x, n_rows):
    sel = (lax.broadcasted_iota(jnp.int32, (n_rows, LANES), 0)
           == lax.broadcasted_iota(jnp.int32, (n_rows, LANES), 1)).astype(BF16)
    nt = functools.partial(lax.dot_general, dimension_numbers=(((1,), (1,)), ((), ())), preferred_element_type=F32)
    hi, mid, lo = _split3(x)
    return nt(sel, hi) + (nt(sel, mid) + nt(sel, lo))


def _tri_iota(c):
    return lax.broadcasted_iota(jnp.int32, (c, c), 0), lax.broadcasted_iota(jnp.int32, (c, c), 1)


def _neumann(m, c):
    ri, ci = _tri_iota(c)
    s = (ri == ci).astype(F32) + m
    p = m
    for _ in range(int(math.log2(c)) - 1):
        p = _mm3(p, p)
        s = s + _mm3(s, p)
    return s


def _shift_rows(x, n):
    return pltpu.roll(x, n % x.shape[0], 0)


def _conv3(x, prev_row, next_row, w):
    rows = x.shape[0]
    ridx = lax.broadcasted_iota(jnp.int32, x.shape, 0)
    up = jnp.where(ridx == 0, prev_row, _shift_rows(x, 1))
    dn = jnp.where(ridx == rows - 1, next_row, _shift_rows(x, -1))
    return up * w[0:1, :] + x * w[1:2, :] + dn * w[2:3, :]


def _head_sum(x, width):
    n = x.shape[1]
    ri, ci = _tri_iota(n)
    return _mm_exact_rhs(x, (ri // width == ci // width).astype(F32))


def _mod_kernel(c_ref, w_ref, b_ref, o_ref):
    o_ref[...] = _mm3(_silu(c_ref[...]), w_ref[...]) + b_ref[...]


def _modulation(c8, w_mod, b_mod):
    n_layers, d = w_mod.shape[0], w_mod.shape[1]
    return pl.pallas_call(
        _mod_kernel,
        grid=(n_layers, 6),
        in_specs=[
            pl.BlockSpec((8, d), lambda l, j: (0, 0)),
            pl.BlockSpec((None, d, d), lambda l, j: (l, 0, j)),
            pl.BlockSpec((None, None, 1, d), lambda l, j: (l, j, 0, 0)),
        ],
        out_specs=pl.BlockSpec((None, None, 8, d), lambda l, j: (l, j, 0, 0)),
        out_shape=jax.ShapeDtypeStruct((n_layers, 6, 8, d), F32),
        compiler_params=_cparams(2),
        name="adaln_mod",
    )(c8, w_mod, b_mod.reshape(n_layers, 6, 1, d))


def _mod_spec(layer, which, rows_per_tile, seq, n_batch):
    base = (layer * 6 + which) * 8
    tiles_per_batch = seq // rows_per_tile

    def imap(i, *_):
        return (base + jnp.minimum(i // tiles_per_batch, n_batch), 0, 0)

    return pl.BlockSpec((None, 1, D_MODEL), imap)


def _inproj_kernel(x_ref, nw_ref, sh_ref, sc_ref, *refs):
    n = len(refs) // 2
    x = x_ref[...]
    h = x * lax.rsqrt(jnp.mean(x * x, axis=-1, keepdims=True) + NORM_EPS) * nw_ref[...]
    h = (h * (1.0 + sc_ref[...]) + sh_ref[...]).astype(BF16)
    for w_ref, o_ref in zip(refs[:n], refs[n:]):
        o_ref[...] = jnp.dot(h, w_ref[...], preferred_element_type=F32)


def _inproj(x_all, norm_w, mod3, layer, weights, seq, n_batch):
    ntok, d = x_all.shape
    widths = [w.shape[1] for w in weights]
    row = lambda wd: pl.BlockSpec((IN_TM, wd), lambda i: (i, 0))
    return pl.pallas_call(
        _inproj_kernel,
        grid=(ntok // IN_TM,),
        in_specs=[row(d), pl.BlockSpec((1, d), lambda i: (0, 0)),
                  _mod_spec(layer, 0, IN_TM, seq, n_batch), _mod_spec(layer, 1, IN_TM, seq, n_batch)]
                 + [pl.BlockSpec(w.shape, lambda i: (0, 0)) for w in weights],
        out_specs=[row(wd) for wd in widths],
        out_shape=[jax.ShapeDtypeStruct((ntok, wd), F32) for wd in widths],
        compiler_params=_cparams(1),
        name="norm_mod_inproj",
    )(x_all, norm_w.reshape(1, d), mod3, mod3, *weights)


def _chunk_maps(n_lat_rows, seq, ctx_len, chunk):
    lat_chunks, ctx_chunks = seq // chunk, ctx_len // chunk
    ctx_base = n_lat_rows // chunk

    def fwd(b, t):
        return jnp.where(t < ctx_chunks, ctx_base + b * ctx_chunks + t, b * lat_chunks + (t - ctx_chunks))

    def bwd(b, t):
        return jnp.where(t < ctx_chunks, ctx_base + b * ctx_chunks + (ctx_chunks - 1 - t),
                         b * lat_chunks + (lat_chunks - 1 - (t - ctx_chunks)))

    return fwd, bwd, lat_chunks + ctx_chunks


def _ssd_prep_kernel(tiles_per_batch, n_lat_tiles, x_ref, p_ref, n_ref, w_ref, b_ref, o_ref):
    i = pl.program_id(0)
    lat = i < n_lat_tiles
    is_start = jnp.logical_or(jnp.logical_not(lat), i % tiles_per_batch == 0)
    is_end = jnp.logical_or(jnp.logical_not(lat), i % tiles_per_batch == tiles_per_batch - 1)
    prev_row = jnp.where(is_start, 0.0, p_ref[7:8, :])
    next_row = jnp.where(is_end, 0.0, n_ref[0:1, :])
    o_ref[...] = _silu(_conv3(x_ref[...], prev_row, next_row, w_ref[...]) + b_ref[...])


def _ssd_prep(p_m, conv_w, conv_b, seq, n_lat, ctx_len):
    ntok = p_m.shape[0]
    wd = M_CONV_DIM
    assert ctx_len == CONV_TM and seq % CONV_TM == 0
    sub = CONV_TM // 8
    last8 = ntok // 8 - 1
    kern = functools.partial(_ssd_prep_kernel, seq // CONV_TM, n_lat // CONV_TM)
    return pl.pallas_call(
        kern,
        grid=(ntok // CONV_TM,),
        in_specs=[
            pl.BlockSpec((CONV_TM, wd), lambda i: (i, 0)),
            pl.BlockSpec((8, wd), lambda i: (jnp.maximum(i * sub - 1, 0), 0)),
            pl.BlockSpec((8, wd), lambda i: (jnp.minimum((i + 1) * sub, last8), 0)),
            pl.BlockSpec((3, wd), lambda i: (0, 0)),
            pl.BlockSpec((1, wd), lambda i: (0, 0)),
        ],
        out_specs=pl.BlockSpec((CONV_TM, wd), lambda i: (i, 0)),
        out_shape=jax.ShapeDtypeStruct((ntok, wd), F32),
        compiler_params=_cparams(1),
        name="ssd_conv_silu",
    )(p_m, p_m, p_m, conv_w, conv_b.reshape(1, wd))


def _ssd_scan_kernel(xf, bf, cf, sf, xb, bb, cb, sb, par_ref, yf_ref, yb_ref, st_ref):
    q = M_CHUNK

    @pl.when(pl.program_id(1) == 0)
    def _():
        st_ref[...] = jnp.zeros_like(st_ref)

    ri, ci = _tri_iota(q)
    for d, (x_ref, b_ref, c_ref, s_ref, y_ref) in enumerate(((xf, bf, cf, sf, yf_ref), (xb, bb, cb, sb, yb_ref))):
        mask = (ri >= ci) if d == 0 else (ri <= ci)
        last = q - 1 if d == 0 else 0
        dt = _softplus(s_ref[...] + par_ref[0:1, :])
        la = dt * par_ref[1:2, :]
        cum = _mm_exact_lhs(mask.astype(F32), la)
        cum_rows = _to_rows(cum, 8)
        xs, bm, cm = x_ref[...], b_ref[...], c_ref[...]
        gmat = [_mm_nt(cm[:, g * M_STATE:(g + 1) * M_STATE], bm[:, g * M_STATE:(g + 1) * M_STATE])
                for g in range(M_GROUPS)]
        outs = []
        for h in range(M_HEADS):
            c = d * M_HEADS + h
            g = h // (M_HEADS // M_GROUPS)
            cc = cum[:, c:c + 1]
            cr = cum_rows[c:c + 1, :]
            seg = jnp.where(mask, jnp.exp(jnp.minimum(cc - cr, 0.0)), 0.0)
            xin = xs[:, h * M_HEAD_DIM:(h + 1) * M_HEAD_DIM] * dt[:, c:c + 1]
            bm_g = bm[:, g * M_STATE:(g + 1) * M_STATE]
            cm_g = cm[:, g * M_STATE:(g + 1) * M_STATE]
            state = st_ref[d, h]
            outs.append(_mm(gmat[g] * seg, xin) + _mm(cm_g, state) * jnp.exp(cc))
            lc = cum[last:last + 1, c:c + 1]
            st_ref[d, h] = jnp.exp(lc) * state + _mm_tn(bm_g * jnp.exp(lc - cc), xin)
        y_ref[...] = jnp.concatenate(outs, axis=1)


def _ssd_scan(xbc, small, par, n_batch, seq, ctx_len):
    ntok = xbc.shape[0]
    q = M_CHUNK
    fwd, bwd, steps = _chunk_maps(n_batch * seq, seq, ctx_len, q)

    def specs(m):
        return [pl.BlockSpec((q, M_INNER), lambda b, t: (m(b, t), 0)),
                pl.BlockSpec((q, M_INNER), lambda b, t: (m(b, t), 1)),
                pl.BlockSpec((q, M_INNER), lambda b, t: (m(b, t), 2)),
                pl.BlockSpec((q, LANES), lambda b, t: (m(b, t), 0))]

    return pl.pallas_call(
        _ssd_scan_kernel,
        grid=(n_batch, steps),
        in_specs=specs(fwd) + specs(bwd) + [pl.BlockSpec((8, LANES), lambda b, t: (0, 0))],
        out_specs=[pl.BlockSpec((q, M_INNER), lambda b, t: (fwd(b, t), 0)),
                   pl.BlockSpec((q, M_INNER), lambda b, t: (bwd(b, t), 0))],
        out_shape=[jax.ShapeDtypeStruct((ntok, M_INNER), F32)] * 2,
        scratch_shapes=[pltpu.VMEM((2, M_HEADS, M_STATE, M_HEAD_DIM), F32)],
        compiler_params=_cparams(2),
        name="ssd_scan",
    )(xbc, xbc, xbc, small, xbc, xbc, xbc, small, par)


def _dn_norm(x, which):
    if which == 2:
        return x
    outs = []
    for h in range(DN_HEADS):
        xh = x[:, h * DN_HEAD_K:(h + 1) * DN_HEAD_K]
        n = lax.rsqrt(jnp.sum(xh * xh, axis=-1, keepdims=True) + 1e-6)
        if which == 0:
            n = n * (DN_HEAD_K ** -0.5)
        outs.append(xh * n)
    return jnp.concatenate(outs, axis=1)


def _dn_prep_ctx_kernel(q_ref, k_ref, v_ref, w_ref, oq_ref, ok_ref, ov_ref):
    for which, (i_ref, o_ref) in enumerate(((q_ref, oq_ref), (k_ref, ok_ref), (v_ref, ov_ref))):
        x = i_ref[...]
        zero = jnp.zeros((1, x.shape[1]), F32)
        w = w_ref[:, which * DN_KD:(which + 1) * DN_KD]
        o_ref[...] = _dn_norm(_silu(_conv3(x, zero, zero, w)), which)


def _dn_prep_lat_kernel(q_ref, qp_ref, qn_ref, k_ref, kp_ref, kn_ref, v_ref, vp_ref, vn_ref, w_ref,
                        oq_ref, ok_ref, ov_ref):
    g = pl.program_id(1)
    first = g == 0
    last = g == pl.num_programs(1) - 1
    wd = DN_KD
    for which, (i_ref, p_ref, n_ref, o_ref) in enumerate(((q_ref, qp_ref, qn_ref, oq_ref),
                                                         (k_ref, kp_ref, kn_ref, ok_ref),
                                                         (v_ref, vp_ref, vn_ref, ov_ref))):
        x = i_ref[...]
        rows = x.shape[0]
        ridx = lax.broadcasted_iota(jnp.int32, x.shape, 0)
        xu = _shift_rows(x, 1)
        xd = _shift_rows(x, -1)
        halo_p = jnp.where(first, 0.0, p_ref[7:8, :])
        halo_n = jnp.where(last, 0.0, n_ref[0:1, :])
        up0 = jnp.concatenate([halo_p, xu[0:1, :(DN_G - 1) * wd]], axis=1)
        dn0 = jnp.concatenate([xd[rows - 1:rows, wd:], halo_n], axis=1)
        up = jnp.where(ridx == 0, up0, xu)
        dn = jnp.where(ridx == rows - 1, dn0, xd)
        w = jnp.concatenate([w_ref[:, which * wd:(which + 1) * wd]] * DN_G, axis=1)
        y = _silu(up * w[0:1, :] + x * w[1:2, :] + dn * w[2:3, :])
        for j in range(DN_G):
            o_ref[:, j * wd:(j + 1) * wd] = _dn_norm(y[:, j * wd:(j + 1) * wd], which)


def _dn_prep(pq, pk, pv, conv_w, n_batch, seq, ctx_len):
    ntok = pq.shape[0]
    n_lat = n_batch * seq
    wd = DN_KD
    rows = seq // GRID_W
    assert rows == DN_CHUNK and GRID_W % DN_G == 0 and n_lat % ctx_len == 0
    n_ctx = ntok - n_lat
    cin = pl.BlockSpec((ctx_len, wd), lambda b: (n_lat // ctx_len + b, 0))
    cout = pl.BlockSpec((ctx_len, wd), lambda b: (b, 0))
    ctx_out = pl.pallas_call(
        _dn_prep_ctx_kernel,
        grid=(n_batch,),
        in_specs=[cin, cin, cin, pl.BlockSpec((3, 3 * wd), lambda b: (0, 0))],
        out_specs=[cout, cout, cout],
        out_shape=[jax.ShapeDtypeStruct((n_ctx, wd), F32)] * 3,
        compiler_params=_cparams(1),
        name="dn_prep_ctx",
    )(pq, pk, pv, conv_w)
    view = lambda a: a.reshape(ntok // GRID_W, GRID_W * wd)
    main = pl.BlockSpec((rows, DN_G * wd), lambda b, g: (b, g))
    prev = pl.BlockSpec((8, wd), lambda b, g: (b * (rows // 8) + rows // 8 - 1, jnp.maximum(g * DN_G - 1, 0)))
    nxt = pl.BlockSpec((8, wd), lambda b, g: (b * (rows // 8), jnp.minimum((g + 1) * DN_G, GRID_W - 1)))
    lat_out = pl.pallas_call(
        _dn_prep_lat_kernel,
        grid=(n_batch, GRID_W // DN_G),
        in_specs=[main, prev, nxt] * 3 + [pl.BlockSpec((3, 3 * wd), lambda b, g: (0, 0))],
        out_specs=[main] * 3,
        out_shape=[jax.ShapeDtypeStruct((n_lat // GRID_W, GRID_W * wd), F32)] * 3,
        compiler_params=_cparams(2),
        name="dn_prep_lat",
    )(view(pq), view(pq), view(pq), view(pk), view(pk), view(pk), view(pv), view(pv), view(pv), conv_w)
    return lat_out, ctx_out


def _dn_scan_kernel(qf, kf, vf, sf, qb, kb, vb, sb, par_ref, s0_ref, of_ref, ob_ref, s1_ref, st_ref):
    c = DN_CHUNK
    t = pl.program_id(1)

    @pl.when(t == 0)
    def _():
        st_ref[...] = s0_ref[...]

    ri, ci = _tri_iota(c)
    for d, (q_ref, k_ref, v_ref, s_ref, o_ref) in enumerate(((qf, kf, vf, sf, of_ref), (qb, kb, vb, sb, ob_ref))):
        incl = (ri >= ci) if d == 0 else (ri <= ci)
        strict = (ri > ci) if d == 0 else (ri < ci)
        last = c - 1 if d == 0 else 0
        sm = s_ref[...]
        beta_all = _sigmoid(sm)
        gl = _softplus(sm + par_ref[0:1, :]) * par_ref[1:2, :]
        cum = _mm_exact_lhs(incl.astype(F32), gl)
        cum_rows = _to_rows(cum, 32)
        qa, ka, va = q_ref[...], k_ref[...], v_ref[...]
        outs = []
        for h in range(DN_HEADS):
            lb = 8 + d * DN_HEADS + h
            lg = 16 + d * DN_HEADS + h
            sl = slice(h * DN_HEAD_K, (h + 1) * DN_HEAD_K)
            qh, kh, vh = qa[:, sl], ka[:, sl], va[:, sl]
            beta = beta_all[:, lb:lb + 1]
            cc = cum[:, lg:lg + 1]
            cr = cum_rows[lg:lg + 1, :]
            decay = jnp.where(incl, jnp.exp(jnp.minimum(cc - cr, 0.0)), 0.0)
            kbeta = kh * beta
            lower = jnp.where(strict, _mm_nt(kbeta, kh) * decay, 0.0)
            tinv = _neumann(-lower, c)
            ecc = jnp.exp(cc)
            u = _mm(tinv, vh * beta)
            w = _mm(tinv, kbeta * ecc)
            attn = _mm_nt(qh, kh) * decay
            state = st_ref[d, h]
            v_new = u - _mm(w, state)
            outs.append(_mm(qh * ecc, state) + _mm(attn, v_new))
            lc = cum[last:last + 1, lg:lg + 1]
            st_ref[d, h] = jnp.exp(lc) * state + _mm_tn(kh * jnp.exp(lc - cc), v_new)
        o_ref[...] = jnp.concatenate(outs, axis=1)

    @pl.when(t == pl.num_programs(1) - 1)
    def _():
        s1_ref[...] = st_ref[...]


def _dn_scan(lat_qkv, ctx_qkv, small, par, n_batch, seq, ctx_len):
    ntok = small.shape[0]
    n_lat = n_batch * seq
    c = DN_CHUNK
    wd = DN_KD
    st_shape = (n_batch, 2, DN_HEADS, DN_HEAD_K, DN_HEAD_V)
    st_spec = pl.BlockSpec((None, 2, DN_HEADS, DN_HEAD_K, DN_HEAD_V), lambda b, t: (b, 0, 0, 0, 0))
    par_spec = pl.BlockSpec((8, LANES), lambda b, t: (0, 0))

    def run(arrs, sm, s0, nchunks, blk_of, sm_blk_of, name):
        fw = lambda f: (lambda b, t: f(b, t))
        bw = lambda f: (lambda b, t: f(b, nchunks - 1 - t))
        ins = lambda o: [pl.BlockSpec((c, wd), o(blk_of))] * 3 + [pl.BlockSpec((c, LANES), o(sm_blk_of))]
        return pl.pallas_call(
            _dn_scan_kernel,
            grid=(n_batch, nchunks),
            in_specs=ins(fw) + ins(bw) + [par_spec, st_spec],
            out_specs=[pl.BlockSpec((c, wd), fw(blk_of)), pl.BlockSpec((c, wd), bw(blk_of)), st_spec],
            out_shape=[jax.ShapeDtypeStruct(arrs[0].shape, F32)] * 2 + [jax.ShapeDtypeStruct(st_shape, F32)],
            scratch_shapes=[pltpu.VMEM((2, DN_HEADS, DN_HEAD_K, DN_HEAD_V), F32)],
            compiler_params=_cparams(2),
            name=name,
        )(*arrs, sm, *arrs, sm, par, s0)

    ctx_chunks = ctx_len // c
    ctx_base = n_lat // c
    ocf, ocb, s_ctx = run(ctx_qkv, small, jnp.zeros(st_shape, F32), ctx_chunks,
                          lambda b, j: (b * ctx_chunks + j, 0),
                          lambda b, j: (ctx_base + b * ctx_chunks + j, 0), "dn_scan_ctx")
    olf, olb, _ = run(lat_qkv, small.reshape(ntok // GRID_W, GRID_W * LANES), s_ctx, GRID_W,
                      lambda b, j: (b, j), lambda b, j: (b, j), "dn_scan_lat")
    return olf.reshape(n_lat, wd), olb.reshape(n_lat, wd), ocf, ocb


def _rw_post(x, shifted, mu_ref, vec_ref, a2_ref, g2_ref, w2_ref, o_ref):
    pm = x + (shifted - x) * mu_ref[...]
    r = pm[:, 0:RW_DIM]
    k = pm[:, RW_DIM:2 * RW_DIM]
    v = pm[:, 2 * RW_DIM:3 * RW_DIM]
    base = 3 * RW_DIM
    ad = pm[:, base + 2 * RW_DECAY_LORA:base + 2 * RW_DECAY_LORA + RW_ICLR_LORA]
    gd = pm[:, base + 2 * RW_DECAY_LORA + RW_ICLR_LORA:RW_COLS]
    a0, k_k, k_a, r_k = vec_ref[0:1, :], vec_ref[1:2, :], vec_ref[2:3, :], vec_ref[3:4, :]
    a = _sigmoid(a0 + _mm(ad, a2_ref[...]))
    g = _mm(_sigmoid(gd), g2_ref[...])
    kx = k * k_k
    kk = kx * lax.rsqrt(_head_sum(kx * kx, RW_HEAD) + 1e-6)
    k2 = k * (1.0 + (a - 1.0) * k_a)
    bonus = _head_sum(r * k2 * r_k, RW_HEAD) * v
    cols = [r, k2, v, -kk, kk * a]
    for d in range(2):
        wd_d = pm[:, base + d * RW_DECAY_LORA:base + (d + 1) * RW_DECAY_LORA]
        pre = vec_ref[4 + d:5 + d, :] + _mm(jnp.tanh(wd_d), w2_ref[d])
        cols.append(-jnp.exp(-_softplus(-pre) - 0.5))
    cols += [bonus, g]
    for j, cval in enumerate(cols):
        o_ref[:, j * RW_DIM:(j + 1) * RW_DIM] = cval


def _rw_prep_kernel(tiles_per_batch, n_lat_tiles, x_ref, u_ref, d_ref, mu_ref, vec_ref, a2_ref, g2_ref, w2_ref, o_ref):
    i = pl.program_id(0)
    x = x_ref[...]
    rows = x.shape[0]
    ridx = lax.broadcasted_iota(jnp.int32, x.shape, 0)
    lane = lax.broadcasted_iota(jnp.int32, x.shape, 1)
    before = _shift_rows(x, 1)
    after = _shift_rows(x, -1)
    col = ridx % GRID_W
    left = jnp.where(col == 0, 0.0, before)
    right = jnp.where(col == GRID_W - 1, 0.0, after)
    top = jnp.where(i % tiles_per_batch == 0, 0.0, u_ref[...])
    bot = jnp.where(i % tiles_per_batch == tiles_per_batch - 1, 0.0, d_ref[...])
    up = jnp.concatenate([top, x[:rows - GRID_W, :]], axis=0)
    down = jnp.concatenate([x[GRID_W:, :], bot], axis=0)
    sel = lane % 4
    shifted_lat = jnp.where(sel == 0, left, jnp.where(sel == 1, right, jnp.where(sel == 2, up, down)))
    prev = jnp.where(ridx == 0, 0.0, before)
    nxt = jnp.where(ridx == rows - 1, 0.0, after)
    shifted_ctx = jnp.where(lane % 2 == 0, prev, nxt)
    shifted = jnp.where(i < n_lat_tiles, shifted_lat, shifted_ctx)
    _rw_post(x, shifted, mu_ref, vec_ref, a2_ref, g2_ref, w2_ref, o_ref)


def _rw_prep(p_rw, mu, vec, a2, g2, w2, n_batch, seq, ctx_len):
    ntok, wd = p_rw.shape
    n_lat = n_batch * seq
    wo = RW_NOUT * RW_DIM
    tm = RW_TM
    assert ctx_len == tm and seq % tm == 0 and tm % GRID_W == 0
    per = tm // GRID_W
    last_blk = ntok // GRID_W - 1
    consts = [pl.BlockSpec(a.shape, (lambda i, nd=a.ndim: (0,) * nd)) for a in (mu, vec, a2, g2, w2)]
    return pl.pallas_call(
        functools.partial(_rw_prep_kernel, seq // tm, n_lat // tm),
        grid=(ntok // tm,),
        in_specs=[pl.BlockSpec((tm, wd), lambda i: (i, 0)),
                  pl.BlockSpec((GRID_W, wd), lambda i: (jnp.maximum(i * per - 1, 0), 0)),
                  pl.BlockSpec((GRID_W, wd), lambda i: (jnp.minimum((i + 1) * per, last_blk), 0))] + consts,
        out_specs=pl.BlockSpec((tm, wo), lambda i: (i, 0)),
        out_shape=jax.ShapeDtypeStruct((ntok, wo), F32),
        compiler_params=_cparams(1),
        name="rw_prep",
    )(p_rw, p_rw, p_rw, mu, vec, a2, g2, w2)


def _rw_scan_kernel(rf, kf, vf, af, bf, wf, rb, kb, vb, ab, bb, wb, yf_ref, yb_ref, st_ref):
    c = RW_CHUNK

    @pl.when(pl.program_id(1) == 0)
    def _():
        st_ref[...] = jnp.zeros_like(st_ref)

    ri, ci = _tri_iota(c)
    for d, (refs, y_ref) in enumerate((((rf, kf, vf, af, bf, wf), yf_ref), ((rb, kb, vb, ab, bb, wb), yb_ref))):
        incl = (ri >= ci) if d == 0 else (ri <= ci)
        strict = (ri > ci) if d == 0 else (ri < ci)
        last = c - 1 if d == 0 else 0
        r, k, v, a, b, lw = [ref[...] for ref in refs]
        cum = _mm_exact_lhs(incl.astype(F32), lw)
        lc = cum[last:last + 1, :]
        e_neg = jnp.exp(-cum)
        e_rel = jnp.exp(lc - cum)
        rt = r * jnp.exp(cum)
        at = a * jnp.exp(cum - lw)
        bt, kt = b * e_neg, k * e_neg
        bl, kl = b * e_rel, k * e_rel
        e_last = jnp.exp(lc)
        outs = []
        for h in range(RW_HEADS):
            sl = slice(h * RW_HEAD, (h + 1) * RW_HEAD)
            aab = jnp.where(strict, _mm_nt(at[:, sl], bt[:, sl]), 0.0)
            aak = jnp.where(strict, _mm_nt(at[:, sl], kt[:, sl]), 0.0)
            arb = jnp.where(incl, _mm_nt(rt[:, sl], bt[:, sl]), 0.0)
            ark = jnp.where(incl, _mm_nt(rt[:, sl], kt[:, sl]), 0.0)
            tinv = _neumann(aab, c)
            state = st_ref[d, h]
            vh = v[:, sl]
            u = _mm(tinv, _mm_nt(at[:, sl], state) + _mm(aak, vh))
            outs.append(_mm_nt(rt[:, sl], state) + _mm(arb, u) + _mm(ark, vh))
            st_ref[d, h] = state * e_last[:, sl] + _mm_tn(u, bl[:, sl]) + _mm_tn(vh, kl[:, sl])
        y_ref[...] = jnp.concatenate(outs, axis=1)


def _rw_scan(rwp, n_batch, seq, ctx_len):
    ntok = rwp.shape[0]
    c = RW_CHUNK
    fwd, bwd, steps = _chunk_maps(n_batch * seq, seq, ctx_len, c)

    def specs(m, wcol):
        return [pl.BlockSpec((c, RW_DIM), lambda b, t, j=j: (m(b, t), j)) for j in (0, 1, 2, 3, 4, wcol)]

    return pl.pallas_call(
        _rw_scan_kernel,
        grid=(n_batch, steps),
        in_specs=specs(fwd, 5) + specs(bwd, 6),
        out_specs=[pl.BlockSpec((c, RW_DIM), lambda b, t: (fwd(b, t), 0)),
                   pl.BlockSpec((c, RW_DIM), lambda b, t: (bwd(b, t), 0))],
        out_shape=[jax.ShapeDtypeStruct((ntok, RW_DIM), F32)] * 2,
        scratch_shapes=[pltpu.VMEM((2, RW_HEADS, RW_HEAD, RW_HEAD), F32)],
        compiler_params=_cparams(2),
        name="rw_scan",
    )(*([rwp] * 12))


def _outproj_kernel(n_lat_tiles, x_ref, ymf, ymb, xs_ref, z_ref, odf, odb, ocf, ocb, gate_ref, yrf, yrb, bonus_ref,
                    g_ref, vec_ref, dnw_ref, wout_ref, gmsa_ref, n2w_ref, sh_ref, sc_ref, wr_ref, br_ref,
                    xo_ref, h2_ref, route_ref, cnt_ref, run_ref):
    tm = x_ref.shape[0]

    @pl.when(pl.program_id(0) == 0)
    def _():
        run_ref[...] = jnp.zeros_like(run_ref)

    ym = (xs_ref[...] * vec_ref[0:1, :] + ymf[...] + ymb[...]) * _silu(z_ref[...])
    gw = M_INNER // M_GROUPS
    parts = []
    for g in range(M_GROUPS):
        yg = ym[:, g * gw:(g + 1) * gw]
        parts.append(yg * lax.rsqrt(jnp.mean(yg * yg, axis=-1, keepdims=True) + 1e-5))
    mix_m = jnp.concatenate(parts, axis=1) * vec_ref[1:2, :]
    od = jnp.where(pl.program_id(0) < n_lat_tiles, odf[...] + odb[...], ocf[...] + ocb[...])
    parts = []
    for h in range(DN_HEADS):
        oh = od[:, h * DN_HEAD_V:(h + 1) * DN_HEAD_V]
        parts.append(oh * lax.rsqrt(jnp.mean(oh * oh, axis=-1, keepdims=True) + 1e-6))
    mix_d = jnp.concatenate(parts, axis=1) * dnw_ref[...] * _silu(gate_ref[...])
    yr = yrf[...] + yrb[...]
    mean = _head_sum(yr, RW_HEAD) * (1.0 / RW_HEAD)
    cen = yr - mean
    var = _head_sum(cen * cen, RW_HEAD) * (1.0 / RW_HEAD)
    mix_r = ((cen * lax.rsqrt(var + RW_LN_EPS)) * vec_ref[2:3, :] + vec_ref[3:4, :] + bonus_ref[...]) * g_ref[...]

    attn = (_mm(mix_m, wout_ref[0:M_INNER, :]) + _mm(mix_d, wout_ref[M_INNER:M_INNER + DN_VD, :])
            + _mm(mix_r, wout_ref[M_INNER + DN_VD:, :]))
    xn = x_ref[...] + gmsa_ref[...] * attn
    xo_ref[...] = xn
    h2 = xn * lax.rsqrt(jnp.mean(xn * xn, axis=-1, keepdims=True) + NORM_EPS) * n2w_ref[...]
    h2 = h2 * (1.0 + sc_ref[...]) + sh_ref[...]
    h2_ref[...] = h2

    logits = _mm3(h2, wr_ref[...]) + br_ref[...]
    lane = lax.broadcasted_iota(jnp.int32, (tm, LANES), 1)
    neg = jnp.float32(-jnp.inf)
    lg = jnp.where(lane < N_EXPERTS, logits, neg)
    tops, idxs = [], []
    onehot = jnp.zeros((tm, LANES), F32)
    for _ in range(TOP_K):
        m = jnp.max(lg, axis=-1, keepdims=True)
        idx = jnp.min(jnp.where(lg == m, lane, LANES), axis=-1, keepdims=True)
        sel = lane == idx
        tops.append(m)
        idxs.append(idx)
        lg = jnp.where(sel, neg, lg)
        onehot = onehot + sel.astype(F32)
    ri, ci = _tri_iota(tm)
    before = jnp.dot((ri > ci).astype(BF16), onehot.astype(BF16), preferred_element_type=F32) + run_ref[0:1, :]
    run_ref[0:1, :] = run_ref[0:1, :] + jnp.sum(onehot, axis=0, keepdims=True)
    exps = [jnp.exp(m - tops[0]) for m in tops]
    denom = exps[0] + exps[1] + exps[2] + exps[3]
    route = jnp.zeros((tm, LANES), F32)
    for k in range(TOP_K):
        rank = jnp.sum(jnp.where(lane == idxs[k], before, 0.0), axis=-1, keepdims=True)
        route = jnp.where(lane == k, idxs[k].astype(F32), route)
        route = jnp.where(lane == TOP_K + k, exps[k] / denom, route)
        route = jnp.where(lane == 2 * TOP_K + k, rank, route)
    route_ref[...] = route
    cnt_ref[...] = run_ref[...]


def _outproj(x_all, ssd_y, xbc, p_m, dn_o, p_gate, rw_y, rwp, vec, dnw, w_out, mod3, layer, norm2_w,
             w_router, b_router, n_rows, seq, n_batch):
    d = x_all.shape[1]
    tm = OUT_TM
    row = lambda wd, col=0: pl.BlockSpec((tm, wd), lambda i, col=col: (i, col))
    const = lambda a: pl.BlockSpec(a.shape, lambda i: (0,) * a.ndim)
    n2 = norm2_w.reshape(1, d)
    n_lat_tiles = n_batch * seq // tm
    lat_row = pl.BlockSpec((tm, DN_VD), lambda i: (jnp.minimum(i, n_lat_tiles - 1), 0))
    ctx_row = pl.BlockSpec((tm, DN_VD), lambda i: (jnp.maximum(i - n_lat_tiles, 0), 0))
    ins = [x_all, ssd_y[0], ssd_y[1], xbc, p_m, dn_o[0], dn_o[1], dn_o[2], dn_o[3], p_gate, rw_y[0], rw_y[1],
           rwp, rwp, vec, dnw, w_out, mod3, n2, mod3, mod3, w_router, b_router]
    specs = [row(d), row(M_INNER), row(M_INNER), row(M_INNER, 0), row(M_INNER, M_CONV_DIM // M_INNER),
             lat_row, lat_row, ctx_row, ctx_row, row(DN_VD), row(RW_DIM), row(RW_DIM), row(RW_DIM, 7),
             row(RW_DIM, 8), const(vec), const(dnw), const(w_out), _mod_spec(layer, 2, tm, seq, n_batch),
             const(n2), _mod_spec(layer, 3, tm, seq, n_batch), _mod_spec(layer, 4, tm, seq, n_batch),
             const(w_router), const(b_router)]
    return pl.pallas_call(
        functools.partial(_outproj_kernel, n_lat_tiles),
        grid=(n_rows // tm,),
        in_specs=specs,
        out_specs=[row(d), row(d), row(LANES), pl.BlockSpec((8, LANES), lambda i: (0, 0))],
        out_shape=[jax.ShapeDtypeStruct((n_rows, d), F32), jax.ShapeDtypeStruct((n_rows, d), F32),
                   jax.ShapeDtypeStruct((n_rows, LANES), F32), jax.ShapeDtypeStruct((8, LANES), F32)],
        scratch_shapes=[pltpu.VMEM((8, LANES), F32)],
        compiler_params=_cparams(1),
        name="finish_outproj_router",
    )(*ins)


def _dispatch_kernel(dest_ref, h_ref, zeros_ref, xs_ref, sem):
    del zeros_ref
    tm = h_ref.shape[0]

    def copy(i, k):
        return pltpu.make_async_copy(h_ref.at[pl.ds(i, 1)], xs_ref.at[pl.ds(dest_ref[0, i * TOP_K + k], 1)], sem)

    def start(i, carry):
        for k in range(TOP_K):
            copy(i, k).start()
        return carry

    def wait(i, carry):
        for k in range(TOP_K):
            copy(i, k).wait()
        return carry

    lax.fori_loop(0, tm, start, 0)
    lax.fori_loop(0, tm, wait, 0)


def _dispatch(h2, dest, n_slots):
    n, d = h2.shape
    tm = MOE_TM
    return pl.pallas_call(
        _dispatch_kernel,
        grid=(n // tm,),
        in_specs=[pl.BlockSpec((None, 1, tm * TOP_K), lambda i: (i, 0, 0), memory_space=pltpu.SMEM),
                  pl.BlockSpec((tm, d), lambda i: (i, 0)),
                  pl.BlockSpec(memory_space=pl.ANY)],
        out_specs=pl.BlockSpec(memory_space=pl.ANY),
        out_shape=jax.ShapeDtypeStruct((n_slots, d), F32),
        scratch_shapes=[pltpu.SemaphoreType.DMA(())],
        input_output_aliases={2: 0},
        compiler_params=_cparams(1),
        name="moe_dispatch",
    )(dest.reshape(n // tm, 1, tm * TOP_K), h2, jnp.zeros((n_slots, d), F32))


def _moe_kernel(be_ref, nv_ref, x_ref, wgu_ref, bgu_ref, wdn_ref, bdn_ref, y_ref, wgu_bf, wdn_bf):
    j = pl.program_id(0)

    @pl.when(j < nv_ref[0])
    def _():
        changed = jnp.logical_or(j == 0, be_ref[j] != be_ref[jnp.maximum(j - 1, 0)])

        @pl.when(changed)
        def _():
            wgu_bf[...] = wgu_ref[...].astype(BF16)
            wdn_bf[...] = wdn_ref[...].astype(BF16)

        gu = jnp.dot(x_ref[...].astype(BF16), wgu_bf[...], preferred_element_type=F32) + bgu_ref[...]
        g_ = jnp.minimum(gu[:, :D_FF], SWIGLU_LIMIT)
        u_ = jnp.clip(gu[:, D_FF:], -SWIGLU_LIMIT, SWIGLU_LIMIT)
        act = g_ * _sigmoid(SWIGLU_ALPHA * g_) * (u_ + 1.0)
        y_ref[...] = jnp.dot(act.astype(BF16), wdn_bf[...], preferred_element_type=F32) + bdn_ref[...]

    @pl.when(j >= nv_ref[0])
    def _():
        y_ref[...] = jnp.zeros_like(y_ref)


def _moe(xs, block_expert, n_valid, w_gate_up, b_gate_up, w_down, b_down):
    n_slots, d = xs.shape
    bm = MOE_BLOCK
    n_exp, _, f2 = w_gate_up.shape
    blk = lambda j, be, nv: (jnp.minimum(j, nv[0] - 1), 0)
    grid_spec = pltpu.PrefetchScalarGridSpec(
        num_scalar_prefetch=2,
        grid=(n_slots // bm,),
        in_specs=[pl.BlockSpec((bm, d), blk),
                  pl.BlockSpec((None, d, f2), lambda j, be, nv: (be[j], 0, 0)),
                  pl.BlockSpec((None, 1, f2), lambda j, be, nv: (be[j], 0, 0)),
                  pl.BlockSpec((None, f2 // 2, d), lambda j, be, nv: (be[j], 0, 0)),
                  pl.BlockSpec((None, 1, d), lambda j, be, nv: (be[j], 0, 0))],
        out_specs=pl.BlockSpec((bm, d), lambda j, be, nv: (j, 0)),
        scratch_shapes=[pltpu.VMEM((d, f2), BF16), pltpu.VMEM((f2 // 2, d), BF16)],
    )
    return pl.pallas_call(
        _moe_kernel,
        grid_spec=grid_spec,
        out_shape=jax.ShapeDtypeStruct((n_slots, d), F32),
        compiler_params=_cparams(1),
        name="moe_experts",
    )(block_expert, n_valid, xs, w_gate_up, b_gate_up.reshape(n_exp, 1, f2), w_down, b_down.reshape(n_exp, 1, d))


def _combine_kernel(final_norm, dest_ref, x_ref, route_ref, g_ref, nf_ref, y_ref, o_ref, buf, sem):
    tm = x_ref.shape[0]

    def copy(i, k):
        return pltpu.make_async_copy(y_ref.at[pl.ds(dest_ref[0, i * TOP_K + k], 1)], buf.at[k, pl.ds(i, 1)], sem)

    def start(i, carry):
        for k in range(TOP_K):
            copy(i, k).start()
        return carry

    def wait(i, carry):
        for k in range(TOP_K):
            copy(i, k).wait()
        return carry

    lax.fori_loop(0, tm, start, 0)
    lax.fori_loop(0, tm, wait, 0)
    route = route_ref[...]
    acc = jnp.zeros(x_ref.shape, F32)
    for k in range(TOP_K):
        acc = acc + route[:, TOP_K + k:TOP_K + k + 1] * buf[k]
    out = x_ref[...] + g_ref[...] * acc
    if final_norm:
        out = out * lax.rsqrt(jnp.mean(out * out, axis=-1, keepdims=True) + NORM_EPS) * nf_ref[...]
    o_ref[...] = out


def _combine(x_new, route, dest, y_slots, mod3, layer, norm_f_w, final_norm, seq, n_batch):
    n, d = x_new.shape
    tm = MOE_TM
    row = lambda wd: pl.BlockSpec((tm, wd), lambda i: (i, 0))
    return pl.pallas_call(
        functools.partial(_combine_kernel, final_norm),
        grid=(n // tm,),
        in_specs=[pl.BlockSpec((None, 1, tm * TOP_K), lambda i: (i, 0, 0), memory_space=pltpu.SMEM),
                  row(d), row(LANES), _mod_spec(layer, 5, tm, seq, n_batch),
                  pl.BlockSpec((1, d), lambda i: (0, 0)),
                  pl.BlockSpec(memory_space=pl.ANY)],
        out_specs=row(d),
        out_shape=jax.ShapeDtypeStruct((n, d), F32),
        scratch_shapes=[pltpu.VMEM((TOP_K, tm, d), F32), pltpu.SemaphoreType.DMA(())],
        compiler_params=_cparams(1),
        name="moe_combine",
    )(dest.reshape(n // tm, 1, tm * TOP_K), x_new, route, mod3, norm_f_w.reshape(1, d), y_slots)


def _routing_tables(route, counts, n_slots):
    bm = MOE_BLOCK
    top_e = route[:, :TOP_K].astype(jnp.int32)
    rank = route[:, 2 * TOP_K:3 * TOP_K].astype(jnp.int32)
    cnt = counts[0, :N_EXPERTS].astype(jnp.int32)
    padded = (cnt + bm - 1) // bm * bm
    ends = jnp.cumsum(padded)
    dest = (ends - padded)[top_e] + rank
    n_blocks = n_slots // bm
    n_valid = ends[-1] // bm
    blk = jnp.minimum(jnp.arange(n_blocks, dtype=jnp.int32), n_valid - 1)
    block_expert = jnp.minimum(jnp.searchsorted(ends // bm, blk, side="right"), N_EXPERTS - 1).astype(jnp.int32)
    return dest.reshape(-1), block_expert, n_valid.reshape(1).astype(jnp.int32)


def _pad_cols(a, width):
    return jnp.pad(a, [(0, 0)] * (a.ndim - 1) + [(0, width - a.shape[-1])])


def kernel(x, c, ctx, c_ctx, w_mod, b_mod, norm1_w, norm2_w, w_in, w_out, m_conv_w, m_conv_b, m_dt_bias, m_a_log, m_d, m_norm_w, dn_conv_w, dn_dt_bias, dn_a_log, dn_norm_w, rw_mu, rw_w0, rw_w2, rw_a0, rw_a2, rw_g2, rw_k_k, rw_k_a, rw_r_k, rw_ln_w, rw_ln_b, w_router, b_router, w_gate_up, b_gate_up, w_down, b_down, norm_f_w):
    n_batch, seq, d = x.shape
    ctx_len = ctx.shape[1]
    depth = w_mod.shape[0]
    n_lat, n_ctx = n_batch * seq, n_batch * ctx_len
    assert d == D_MODEL and seq // GRID_W == DN_CHUNK and n_batch + 1 <= 8

    x_all = jnp.concatenate([x.reshape(n_lat, d), ctx.reshape(n_ctx, d)], axis=0)
    c8 = jnp.zeros((8, d), F32).at[:n_batch].set(c).at[n_batch].set(c_ctx)
    mod3 = _modulation(c8, w_mod, b_mod).reshape(depth * 6 * 8, 1, d)

    out = None
    for i in range(depth):
        last = i == depth - 1
        wi = w_in[i]
        mo, do, ro = 0, M_COLS, M_COLS + DN_COLS
        w_m = jnp.concatenate([wi[:, mo + M_INNER:mo + M_INNER + M_CONV_DIM], wi[:, mo:mo + M_INNER]], axis=1)
        w_q = wi[:, do:do + DN_KD]
        w_k = wi[:, do + DN_KD:do + 2 * DN_KD]
        w_v = wi[:, do + 2 * DN_KD:do + DN_CONV_DIM]
        w_g = wi[:, do + DN_CONV_DIM:do + DN_CONV_DIM + DN_VD]
        w_r = _pad_cols(wi[:, ro:ro + RW_COLS], RW_COLS_PAD)
        w_s = _pad_cols(jnp.concatenate([wi[:, mo + M_INNER + M_CONV_DIM:mo + M_COLS],
                                         wi[:, do + DN_CONV_DIM + DN_VD:do + DN_COLS]], axis=1), LANES)
        weights = [w.astype(BF16) for w in (w_m, w_q, w_k, w_v, w_g, w_r, w_s)]
        p_m, p_q, p_k, p_v, p_gate, p_rw, small = _inproj(x_all, norm1_w[i], mod3, i, weights, seq, n_batch)

        xbc = _ssd_prep(p_m, m_conv_w[i], m_conv_b[i], seq, n_lat, ctx_len)
        ssd_par = jnp.zeros((8, LANES), F32)
        ssd_par = ssd_par.at[0, :2 * M_HEADS].set(m_dt_bias[i].reshape(-1))
        ssd_par = ssd_par.at[1, :2 * M_HEADS].set(-jnp.exp(m_a_log[i].astype(F32)).reshape(-1))
        ssd_y = _ssd_scan(xbc, small, ssd_par, n_batch, seq, ctx_len)

        lat_qkv, ctx_qkv = _dn_prep(p_q, p_k, p_v, dn_conv_w[i], n_batch, seq, ctx_len)
        dn_par = jnp.zeros((8, LANES), F32)
        dn_par = dn_par.at[0, 16:16 + 2 * DN_HEADS].set(dn_dt_bias[i].reshape(-1))
        dn_par = dn_par.at[1, 16:16 + 2 * DN_HEADS].set(-jnp.exp(dn_a_log[i].astype(F32)).reshape(-1))
        dn_o = _dn_scan(lat_qkv, ctx_qkv, small, dn_par, n_batch, seq, ctx_len)

        rw_vec = jnp.stack([rw_a0[i], rw_k_k[i], rw_k_a[i], rw_r_k[i].reshape(-1), rw_w0[i, 0], rw_w0[i, 1],
                            jnp.zeros((RW_DIM,), F32), jnp.zeros((RW_DIM,), F32)])
        rwp = _rw_prep(p_rw, _pad_cols(rw_mu[i], RW_COLS_PAD).reshape(1, -1), rw_vec, rw_a2[i], rw_g2[i], rw_w2[i],
                       n_batch, seq, ctx_len)
        rw_y = _rw_scan(rwp, n_batch, seq, ctx_len)

        n_rows = n_lat if last else n_lat + n_ctx
        fin_vec = jnp.stack([jnp.repeat(m_d[i], M_HEAD_DIM), m_norm_w[i], rw_ln_w[i], rw_ln_b[i]]
                            + [jnp.zeros((RW_DIM,), F32)] * 4)
        dnw = jnp.tile(dn_norm_w[i], DN_HEADS).reshape(1, DN_VD)
        x_new, h2, route, counts = _outproj(
            x_all, ssd_y, xbc, p_m, dn_o, p_gate, rw_y, rwp, fin_vec, dnw, w_out[i].astype(BF16), mod3, i,
            norm2_w[i], _pad_cols(w_router[i], LANES), _pad_cols(b_router[i].reshape(1, -1), LANES),
            n_rows, seq, n_batch)

        n_slots = (n_rows * TOP_K // MOE_BLOCK + N_EXPERTS) * MOE_BLOCK
        dest, block_expert, n_valid = _routing_tables(route, counts, n_slots)
        xs = _dispatch(h2, dest, n_slots)
        y_slots = _moe(xs, block_expert, n_valid, w_gate_up[i], b_gate_up[i], w_down[i], b_down[i])
        res = _combine(x_new, route, dest, y_slots, mod3, i, norm_f_w, last, seq, n_batch)
        if last:
            out = res
        else:
            x_all = res
    return out.reshape(n_batch, seq, d)
```

```python
import functools
import math

import jax
import jax.numpy as jnp
from jax import lax
from jax.experimental import pallas as pl
from jax.experimental.pallas import tpu as pltpu

F32 = jnp.float32
BF16 = jnp.bfloat16

D_MODEL = 1024
GRID_W = 64
NORM_EPS = 1e-6

M_HEADS, M_HEAD_DIM, M_STATE, M_GROUPS = 4, 64, 128, 2
M_INNER = M_HEADS * M_HEAD_DIM
M_CONV_DIM = M_INNER + 2 * M_GROUPS * M_STATE
M_COLS = M_INNER + M_CONV_DIM + 2 * M_HEADS
M_CHUNK = 128

DN_HEADS, DN_HEAD_K, DN_HEAD_V = 4, 128, 128
DN_KD = DN_HEADS * DN_HEAD_K
DN_VD = DN_HEADS * DN_HEAD_V
DN_CONV_DIM = 2 * DN_KD + DN_VD
DN_COLS = DN_CONV_DIM + DN_VD + 4 * DN_HEADS
DN_CHUNK = 64

RW_HEADS, RW_HEAD = 4, 64
RW_DIM = RW_HEADS * RW_HEAD
RW_DECAY_LORA, RW_ICLR_LORA, RW_GATE_LORA = 64, 64, 128
RW_COLS = 3 * RW_DIM + 2 * RW_DECAY_LORA + RW_ICLR_LORA + RW_GATE_LORA
RW_COLS_PAD = 1152
RW_LN_EPS = RW_HEAD * 1e-5
RW_CHUNK = 64
RW_NOUT = 9

N_EXPERTS, TOP_K = 32, 4
D_FF = 1024
SWIGLU_ALPHA, SWIGLU_LIMIT = 1.702, 7.0
MOE_BLOCK = 256

LANES = 128
VMEM_LIMIT = 56 * 1024 * 1024

IN_TM = 512
CONV_TM = 256
DN_G = 8
RW_TM = 256
OUT_TM = 256
MOE_TM = 256
SCAN_SUB = 2


def _cparams(n_axes):
    return pltpu.CompilerParams(dimension_semantics=("arbitrary",) * n_axes, vmem_limit_bytes=VMEM_LIMIT)


def _sigmoid(x):
    return 1.0 / (1.0 + jnp.exp(-x))


def _silu(x):
    return x * _sigmoid(x)


def _softplus(x):
    return jnp.maximum(x, 0.0) + jnp.log(1.0 + jnp.exp(-jnp.abs(x)))


def _mm(a, b):
    return jnp.dot(a.astype(BF16), b.astype(BF16), preferred_element_type=F32)


def _mm_nt(a, b):
    return lax.dot_general(a.astype(BF16), b.astype(BF16), (((1,), (1,)), ((), ())), preferred_element_type=F32)


def _mm_tn(a, b):
    return lax.dot_general(a.astype(BF16), b.astype(BF16), (((0,), (0,)), ((), ())), preferred_element_type=F32)


def _split3(x):
    hi = x.astype(BF16)
    r1 = x - hi.astype(F32)
    mid = r1.astype(BF16)
    lo = (r1 - mid.astype(F32)).astype(BF16)
    return hi, mid, lo


def _mm3(a, b):
    ah, am, _ = _split3(a)
    bh, bm, _ = _split3(b)
    dot = functools.partial(jnp.dot, preferred_element_type=F32)
    return dot(ah, bh) + (dot(am, bh) + dot(ah, bm))


def _mm_exact_lhs(a01, x):
    a = a01.astype(BF16)
    dot = functools.partial(jnp.dot, preferred_element_type=F32)
    hi, mid, lo = _split3(x)
    return dot(a, hi) + (dot(a, mid) + dot(a, lo))


def _mm_exact_rhs(x, b01):
    b = b01.astype(BF16)
    dot = functools.partial(jnp.dot, preferred_element_type=F32)
    hi, mid, lo = _split3(x)
    return dot(hi, b) + (dot(mid, b) + dot(lo, b))


def _to_rows(x, n_rows):
    sel = (lax.broadcasted_iota(jnp.int32, (n_rows, LANES), 0)
           == lax.broadcasted_iota(jnp.int32, (n_rows, LANES), 1)).astype(BF16)
    nt = functools.partial(lax.dot_general, dimension_numbers=(((1,), (1,)), ((), ())), preferred_element_type=F32)
    hi, mid, lo = _split3(x)
    return nt(sel, hi) + (nt(sel, mid) + nt(sel, lo))


def _tri_iota(c):
    return lax.broadcasted_iota(jnp.int32, (c, c), 0), lax.broadcasted_iota(jnp.int32, (c, c), 1)


def _neumann(m, nil):
    ri, ci = _tri_iota(m.shape[0])
    s = (ri == ci).astype(F32) + m
    p = m
    for _ in range(int(math.log2(nil)) - 1):
        p = _mm(p, p)
        s = s + _mm(s, p)
    return s


def _stack_rows(x, n, width):
    return jnp.concatenate([x[:, h * width:(h + 1) * width] for h in range(n)], axis=0)


def _block_tri(n_blocks, c, reverse):
    ri, ci = _tri_iota(n_blocks * c)
    same = (ri // c) == (ci // c)
    if reverse:
        return jnp.logical_and(same, ri <= ci), jnp.logical_and(same, ri < ci)
    return jnp.logical_and(same, ri >= ci), jnp.logical_and(same, ri > ci)


def _shift_rows(x, n):
    return pltpu.roll(x, n % x.shape[0], 0)


def _conv3(x, prev_row, next_row, w):
    rows = x.shape[0]
    ridx = lax.broadcasted_iota(jnp.int32, x.shape, 0)
    up = jnp.where(ridx == 0, prev_row, _shift_rows(x, 1))
    dn = jnp.where(ridx == rows - 1, next_row, _shift_rows(x, -1))
    return up * w[0:1, :] + x * w[1:2, :] + dn * w[2:3, :]


def _head_sum(x, width):
    n = x.shape[1]
    ri, ci = _tri_iota(n)
    return _mm_exact_rhs(x, (ri // width == ci // width).astype(F32))


def _mod_kernel(c_ref, w_ref, b_ref, o_ref):
    o_ref[...] = _mm3(_silu(c_ref[...]), w_ref[...]) + b_ref[...]


def _modulation(c8, w_mod, b_mod):
    n_layers, d = w_mod.shape[0], w_mod.shape[1]
    return pl.pallas_call(
        _mod_kernel,
        grid=(n_layers, 6),
        in_specs=[
            pl.BlockSpec((8, d), lambda l, j: (0, 0)),
            pl.BlockSpec((None, d, d), lambda l, j: (l, 0, j)),
            pl.BlockSpec((None, None, 1, d), lambda l, j: (l, j, 0, 0)),
        ],
        out_specs=pl.BlockSpec((None, None, 8, d), lambda l, j: (l, j, 0, 0)),
        out_shape=jax.ShapeDtypeStruct((n_layers, 6, 8, d), F32),
        compiler_params=_cparams(2),
        name="adaln_mod",
    )(c8, w_mod, b_mod.reshape(n_layers, 6, 1, d))


def _mod_spec(layer, which, rows_per_tile, seq, n_batch):
    base = (layer * 6 + which) * 8
    tiles_per_batch = seq // rows_per_tile

    def imap(i, *_):
        return (base + jnp.minimum(i // tiles_per_batch, n_batch), 0, 0)

    return pl.BlockSpec((None, 1, D_MODEL), imap)


def _inproj_kernel(x_ref, nw_ref, sh_ref, sc_ref, *refs):
    n = len(refs) // 2
    x = x_ref[...]
    h = x * lax.rsqrt(jnp.mean(x * x, axis=-1, keepdims=True) + NORM_EPS) * nw_ref[...]
    h = (h * (1.0 + sc_ref[...]) + sh_ref[...]).astype(BF16)
    for w_ref, o_ref in zip(refs[:n], refs[n:]):
        o_ref[...] = jnp.dot(h, w_ref[...], preferred_element_type=F32)


def _inproj(x_all, norm_w, mod3, layer, weights, seq, n_batch):
    ntok, d = x_all.shape
    widths = [w.shape[1] for w in weights]
    row = lambda wd: pl.BlockSpec((IN_TM, wd), lambda i: (i, 0))
    return pl.pallas_call(
        _inproj_kernel,
        grid=(ntok // IN_TM,),
        in_specs=[row(d), pl.BlockSpec((1, d), lambda i: (0, 0)),
                  _mod_spec(layer, 0, IN_TM, seq, n_batch), _mod_spec(layer, 1, IN_TM, seq, n_batch)]
                 + [pl.BlockSpec(w.shape, lambda i: (0, 0)) for w in weights],
        out_specs=[row(wd) for wd in widths],
        out_shape=[jax.ShapeDtypeStruct((ntok, wd), F32) for wd in widths],
        compiler_params=_cparams(1),
        name="norm_mod_inproj",
    )(x_all, norm_w.reshape(1, d), mod3, mod3, *weights)


def _chunk_maps(n_lat_rows, seq, ctx_len, chunk):
    lat_chunks, ctx_chunks = seq // chunk, ctx_len // chunk
    ctx_base = n_lat_rows // chunk

    def fwd(b, t):
        return jnp.where(t < ctx_chunks, ctx_base + b * ctx_chunks + t, b * lat_chunks + (t - ctx_chunks))

    def bwd(b, t):
        return jnp.where(t < ctx_chunks, ctx_base + b * ctx_chunks + (ctx_chunks - 1 - t),
                         b * lat_chunks + (lat_chunks - 1 - (t - ctx_chunks)))

    return fwd, bwd, lat_chunks + ctx_chunks


def _ssd_prep_kernel(tiles_per_batch, n_lat_tiles, x_ref, p_ref, n_ref, w_ref, b_ref, o_ref):
    i = pl.program_id(0)
    lat = i < n_lat_tiles
    is_start = jnp.logical_or(jnp.logical_not(lat), i % tiles_per_batch == 0)
    is_end = jnp.logical_or(jnp.logical_not(lat), i % tiles_per_batch == tiles_per_batch - 1)
    prev_row = jnp.where(is_start, 0.0, p_ref[7:8, :])
    next_row = jnp.where(is_end, 0.0, n_ref[0:1, :])
    o_ref[...] = _silu(_conv3(x_ref[...], prev_row, next_row, w_ref[...]) + b_ref[...])


def _ssd_prep(p_m, conv_w, conv_b, seq, n_lat, ctx_len):
    ntok = p_m.shape[0]
    wd = M_CONV_DIM
    assert ctx_len == CONV_TM and seq % CONV_TM == 0
    sub = CONV_TM // 8
    last8 = ntok // 8 - 1
    kern = functools.partial(_ssd_prep_kernel, seq // CONV_TM, n_lat // CONV_TM)
    return pl.pallas_call(
        kern,
        grid=(ntok // CONV_TM,),
        in_specs=[
            pl.BlockSpec((CONV_TM, wd), lambda i: (i, 0)),
            pl.BlockSpec((8, wd), lambda i: (jnp.maximum(i * sub - 1, 0), 0)),
            pl.BlockSpec((8, wd), lambda i: (jnp.minimum((i + 1) * sub, last8), 0)),
            pl.BlockSpec((3, wd), lambda i: (0, 0)),
            pl.BlockSpec((1, wd), lambda i: (0, 0)),
        ],
        out_specs=pl.BlockSpec((CONV_TM, wd), lambda i: (i, 0)),
        out_shape=jax.ShapeDtypeStruct((ntok, wd), F32),
        compiler_params=_cparams(1),
        name="ssd_conv_silu",
    )(p_m, p_m, p_m, conv_w, conv_b.reshape(1, wd))


def _ssd_scan_kernel(xf, bf, cf, sf, xb, bb, cb, sb, par_ref, yf_ref, yb_ref, st_ref):
    q = M_CHUNK

    @pl.when(pl.program_id(1) == 0)
    def _():
        st_ref[...] = jnp.zeros_like(st_ref)

    ri, ci = _tri_iota(q)
    for d, (x_ref, b_ref, c_ref, s_ref, y_ref) in enumerate(((xf, bf, cf, sf, yf_ref), (xb, bb, cb, sb, yb_ref))):
        mask = (ri >= ci) if d == 0 else (ri <= ci)
        last = q - 1 if d == 0 else 0
        dt = _softplus(s_ref[...] + par_ref[0:1, :])
        la = dt * par_ref[1:2, :]
        cum = _mm_exact_lhs(mask.astype(F32), la)
        cum_rows = _to_rows(cum, 8)
        xs, bm, cm = x_ref[...], b_ref[...], c_ref[...]
        gmat = [_mm_nt(cm[:, g * M_STATE:(g + 1) * M_STATE], bm[:, g * M_STATE:(g + 1) * M_STATE])
                for g in range(M_GROUPS)]
        outs = []
        for h in range(M_HEADS):
            c = d * M_HEADS + h
            g = h // (M_HEADS // M_GROUPS)
            cc = cum[:, c:c + 1]
            cr = cum_rows[c:c + 1, :]
            seg = jnp.where(mask, jnp.exp(jnp.minimum(cc - cr, 0.0)), 0.0)
            xin = xs[:, h * M_HEAD_DIM:(h + 1) * M_HEAD_DIM] * dt[:, c:c + 1]
            bm_g = bm[:, g * M_STATE:(g + 1) * M_STATE]
            cm_g = cm[:, g * M_STATE:(g + 1) * M_STATE]
            state = st_ref[d, h]
            outs.append(_mm(gmat[g] * seg, xin) + _mm(cm_g, state) * jnp.exp(cc))
            lc = cum[last:last + 1, c:c + 1]
            st_ref[d, h] = jnp.exp(lc) * state + _mm_tn(bm_g * jnp.exp(lc - cc), xin)
        y_ref[...] = jnp.concatenate(outs, axis=1)


def _ssd_scan(xbc, small, par, n_batch, seq, ctx_len):
    ntok = xbc.shape[0]
    q = M_CHUNK
    fwd, bwd, steps = _chunk_maps(n_batch * seq, seq, ctx_len, q)

    def specs(m):
        return [pl.BlockSpec((q, M_INNER), lambda b, t: (m(b, t), 0)),
                pl.BlockSpec((q, M_INNER), lambda b, t: (m(b, t), 1)),
                pl.BlockSpec((q, M_INNER), lambda b, t: (m(b, t), 2)),
                pl.BlockSpec((q, LANES), lambda b, t: (m(b, t), 0))]

    return pl.pallas_call(
        _ssd_scan_kernel,
        grid=(n_batch, steps),
        in_specs=specs(fwd) + specs(bwd) + [pl.BlockSpec((8, LANES), lambda b, t: (0, 0))],
        out_specs=[pl.BlockSpec((q, M_INNER), lambda b, t: (fwd(b, t), 0)),
                   pl.BlockSpec((q, M_INNER), lambda b, t: (bwd(b, t), 0))],
        out_shape=[jax.ShapeDtypeStruct((ntok, M_INNER), F32)] * 2,
        scratch_shapes=[pltpu.VMEM((2, M_HEADS, M_STATE, M_HEAD_DIM), F32)],
        compiler_params=_cparams(2),
        name="ssd_scan",
    )(xbc, xbc, xbc, small, xbc, xbc, xbc, small, par)


def _dn_norm(x, which):
    if which == 2:
        return x
    outs = []
    for h in range(DN_HEADS):
        xh = x[:, h * DN_HEAD_K:(h + 1) * DN_HEAD_K]
        n = lax.rsqrt(jnp.sum(xh * xh, axis=-1, keepdims=True) + 1e-6)
        if which == 0:
            n = n * (DN_HEAD_K ** -0.5)
        outs.append(xh * n)
    return jnp.concatenate(outs, axis=1)


def _dn_prep_ctx_kernel(q_ref, k_ref, v_ref, w_ref, oq_ref, ok_ref, ov_ref):
    for which, (i_ref, o_ref) in enumerate(((q_ref, oq_ref), (k_ref, ok_ref), (v_ref, ov_ref))):
        x = i_ref[...]
        zero = jnp.zeros((1, x.shape[1]), F32)
        w = w_ref[:, which * DN_KD:(which + 1) * DN_KD]
        o_ref[...] = _dn_norm(_silu(_conv3(x, zero, zero, w)), which)


def _dn_prep_lat_kernel(q_ref, qp_ref, qn_ref, k_ref, kp_ref, kn_ref, v_ref, vp_ref, vn_ref, w_ref,
                        oq_ref, ok_ref, ov_ref):
    g = pl.program_id(1)
    first = g == 0
    last = g == pl.num_programs(1) - 1
    wd = DN_KD
    for which, (i_ref, p_ref, n_ref, o_ref) in enumerate(((q_ref, qp_ref, qn_ref, oq_ref),
                                                         (k_ref, kp_ref, kn_ref, ok_ref),
                                                         (v_ref, vp_ref, vn_ref, ov_ref))):
        x = i_ref[...]
        rows = x.shape[0]
        ridx = lax.broadcasted_iota(jnp.int32, x.shape, 0)
        xu = _shift_rows(x, 1)
        xd = _shift_rows(x, -1)
        halo_p = jnp.where(first, 0.0, p_ref[7:8, :])
        halo_n = jnp.where(last, 0.0, n_ref[0:1, :])
        up0 = jnp.concatenate([halo_p, xu[0:1, :(DN_G - 1) * wd]], axis=1)
        dn0 = jnp.concatenate([xd[rows - 1:rows, wd:], halo_n], axis=1)
        up = jnp.where(ridx == 0, up0, xu)
        dn = jnp.where(ridx == rows - 1, dn0, xd)
        w = jnp.concatenate([w_ref[:, which * wd:(which + 1) * wd]] * DN_G, axis=1)
        y = _silu(up * w[0:1, :] + x * w[1:2, :] + dn * w[2:3, :])
        for j in range(DN_G):
            o_ref[:, j * wd:(j + 1) * wd] = _dn_norm(y[:, j * wd:(j + 1) * wd], which)


def _dn_prep(pq, pk, pv, conv_w, n_batch, seq, ctx_len):
    ntok = pq.shape[0]
    n_lat = n_batch * seq
    wd = DN_KD
    rows = seq // GRID_W
    assert rows == DN_CHUNK and GRID_W % DN_G == 0 and n_lat % ctx_len == 0
    n_ctx = ntok - n_lat
    cin = pl.BlockSpec((ctx_len, wd), lambda b: (n_lat // ctx_len + b, 0))
    cout = pl.BlockSpec((ctx_len, wd), lambda b: (b, 0))
    ctx_out = pl.pallas_call(
        _dn_prep_ctx_kernel,
        grid=(n_batch,),
        in_specs=[cin, cin, cin, pl.BlockSpec((3, 3 * wd), lambda b: (0, 0))],
        out_specs=[cout, cout, cout],
        out_shape=[jax.ShapeDtypeStruct((n_ctx, wd), F32)] * 3,
        compiler_params=_cparams(1),
        name="dn_prep_ctx",
    )(pq, pk, pv, conv_w)
    view = lambda a: a.reshape(ntok // GRID_W, GRID_W * wd)
    main = pl.BlockSpec((rows, DN_G * wd), lambda b, g: (b, g))
    prev = pl.BlockSpec((8, wd), lambda b, g: (b * (rows // 8) + rows // 8 - 1, jnp.maximum(g * DN_G - 1, 0)))
    nxt = pl.BlockSpec((8, wd), lambda b, g: (b * (rows // 8), jnp.minimum((g + 1) * DN_G, GRID_W - 1)))
    lat_out = pl.pallas_call(
        _dn_prep_lat_kernel,
        grid=(n_batch, GRID_W // DN_G),
        in_specs=[main, prev, nxt] * 3 + [pl.BlockSpec((3, 3 * wd), lambda b, g: (0, 0))],
        out_specs=[main] * 3,
        out_shape=[jax.ShapeDtypeStruct((n_lat // GRID_W, GRID_W * wd), F32)] * 3,
        compiler_params=_cparams(2),
        name="dn_prep_lat",
    )(view(pq), view(pq), view(pq), view(pk), view(pk), view(pk), view(pv), view(pv), view(pv), conv_w)
    return lat_out, ctx_out


def _dn_scan_kernel(side_by_side, qf, kf, vf, sf, qb, kb, vb, sb, par_ref, s0_ref, of_ref, ob_ref, s1_ref, st_ref):
    c = DN_CHUNK
    t = pl.program_id(1)

    @pl.when(t == 0)
    def _():
        st_ref[...] = s0_ref[...]

    def sub(ref, s, width):
        return ref[:, s * width:(s + 1) * width] if side_by_side else ref[s * c:(s + 1) * c, :]

    ri, ci = _tri_iota(c)
    for d, (q_ref, k_ref, v_ref, s_ref, o_ref) in enumerate(((qf, kf, vf, sf, of_ref), (qb, kb, vb, sb, ob_ref))):
        incl_c = (ri >= ci) if d == 0 else (ri <= ci)
        incl, strict = _block_tri(DN_HEADS, c, d == 1)
        last = c - 1 if d == 0 else 0
        lb, lg = 8 + d * DN_HEADS, 16 + d * DN_HEADS
        pre = []
        for s in range(SCAN_SUB):
            sm = sub(s_ref, s, LANES)
            beta_all = _sigmoid(sm)
            gl = _softplus(sm + par_ref[0:1, :]) * par_ref[1:2, :]
            cum = _mm_exact_lhs(incl_c.astype(F32), gl)
            cum_rows = _to_rows(cum, 32)
            beta = jnp.concatenate([beta_all[:, lb + h:lb + h + 1] for h in range(DN_HEADS)], axis=0)
            cc = jnp.concatenate([cum[:, lg + h:lg + h + 1] for h in range(DN_HEADS)], axis=0)
            cr = jnp.concatenate([cum_rows[lg + h:lg + h + 1, :] for h in range(DN_HEADS)], axis=1)
            decay = jnp.where(incl, jnp.exp(jnp.minimum(cc - cr, 0.0)), 0.0)
            qs = _stack_rows(sub(q_ref, s, DN_KD), DN_HEADS, DN_HEAD_K)
            ks = _stack_rows(sub(k_ref, s, DN_KD), DN_HEADS, DN_HEAD_K)
            vs = _stack_rows(sub(v_ref, s, DN_VD), DN_HEADS, DN_HEAD_V)
            kbeta = ks * beta
            lower = jnp.where(strict, _mm_nt(kbeta, ks) * decay, 0.0)
            tinv = _neumann(-lower, c)
            ecc = jnp.exp(cc)
            uw = _mm(tinv, jnp.concatenate([vs * beta, kbeta * ecc], axis=1))
            attn = _mm_nt(qs, ks) * decay
            lcs = [cum[last:last + 1, lg + h:lg + h + 1] for h in range(DN_HEADS)]
            kdec = [ks[h * c:(h + 1) * c] * jnp.exp(lcs[h] - cc[h * c:(h + 1) * c]) for h in range(DN_HEADS)]
            pre.append((uw[:, :DN_HEAD_V], uw[:, DN_HEAD_V:], attn, qs * ecc, kdec, [jnp.exp(lc) for lc in lcs]))
        states = [st_ref[d, h] for h in range(DN_HEADS)]
        for s in (range(SCAN_SUB) if d == 0 else reversed(range(SCAN_SUB))):
            u, w, attn, qe, kdec, e_last = pre[s]
            o_state, v_new = [], []
            for h in range(DN_HEADS):
                rows = slice(h * c, (h + 1) * c)
                qw = _mm(jnp.concatenate([qe[rows], w[rows]], axis=0), states[h])
                o_state.append(qw[:c])
                v_new.append(u[rows] - qw[c:])
            o = jnp.concatenate(o_state, axis=0) + _mm(attn, jnp.concatenate(v_new, axis=0))
            states = [e_last[h] * states[h] + _mm_tn(kdec[h], v_new[h]) for h in range(DN_HEADS)]
            o_cols = jnp.concatenate([o[h * c:(h + 1) * c] for h in range(DN_HEADS)], axis=1)
            if side_by_side:
                o_ref[:, s * DN_VD:(s + 1) * DN_VD] = o_cols
            else:
                o_ref[s * c:(s + 1) * c, :] = o_cols
        for h in range(DN_HEADS):
            st_ref[d, h] = states[h]

    @pl.when(t == pl.num_programs(1) - 1)
    def _():
        s1_ref[...] = st_ref[...]


def _dn_scan(lat_qkv, ctx_qkv, small, par, n_batch, seq, ctx_len):
    ntok = small.shape[0]
    n_lat = n_batch * seq
    c = DN_CHUNK
    wd = DN_KD
    st_shape = (n_batch, 2, DN_HEADS, DN_HEAD_K, DN_HEAD_V)
    st_spec = pl.BlockSpec((None, 2, DN_HEADS, DN_HEAD_K, DN_HEAD_V), lambda b, t: (b, 0, 0, 0, 0))
    par_spec = pl.BlockSpec((8, LANES), lambda b, t: (0, 0))

    def run(arrs, sm, s0, nsteps, side_by_side, blk_of, sm_blk_of, name):
        fw = lambda f: (lambda b, t: f(b, t))
        bw = lambda f: (lambda b, t: f(b, nsteps - 1 - t))
        shp = lambda w: (c, SCAN_SUB * w) if side_by_side else (SCAN_SUB * c, w)
        ins = lambda o: [pl.BlockSpec(shp(wd), o(blk_of))] * 3 + [pl.BlockSpec(shp(LANES), o(sm_blk_of))]
        return pl.pallas_call(
            functools.partial(_dn_scan_kernel, side_by_side),
            grid=(n_batch, nsteps),
            in_specs=ins(fw) + ins(bw) + [par_spec, st_spec],
            out_specs=[pl.BlockSpec(shp(wd), fw(blk_of)), pl.BlockSpec(shp(wd), bw(blk_of)), st_spec],
            out_shape=[jax.ShapeDtypeStruct(arrs[0].shape, F32)] * 2 + [jax.ShapeDtypeStruct(st_shape, F32)],
            scratch_shapes=[pltpu.VMEM((2, DN_HEADS, DN_HEAD_K, DN_HEAD_V), F32)],
            compiler_params=_cparams(2),
            name=name,
        )(*arrs, sm, *arrs, sm, par, s0)

    ctx_steps = ctx_len // (c * SCAN_SUB)
    ctx_base = n_lat // (c * SCAN_SUB)
    ocf, ocb, s_ctx = run(ctx_qkv, small, jnp.zeros(st_shape, F32), ctx_steps, False,
                          lambda b, j: (b * ctx_steps + j, 0),
                          lambda b, j: (ctx_base + b * ctx_steps + j, 0), "dn_scan_ctx")
    olf, olb, _ = run(lat_qkv, small.reshape(ntok // GRID_W, GRID_W * LANES), s_ctx, GRID_W // SCAN_SUB, True,
                      lambda b, j: (b, j), lambda b, j: (b, j), "dn_scan_lat")
    return olf.reshape(n_lat, wd), olb.reshape(n_lat, wd), ocf, ocb


def _rw_post(x, shifted, mu_ref, vec_ref, a2_ref, g2_ref, w2_ref, o_ref):
    pm = x + (shifted - x) * mu_ref[...]
    r = pm[:, 0:RW_DIM]
    k = pm[:, RW_DIM:2 * RW_DIM]
    v = pm[:, 2 * RW_DIM:3 * RW_DIM]
    base = 3 * RW_DIM
    ad = pm[:, base + 2 * RW_DECAY_LORA:base + 2 * RW_DECAY_LORA + RW_ICLR_LORA]
    gd = pm[:, base + 2 * RW_DECAY_LORA + RW_ICLR_LORA:RW_COLS]
    a0, k_k, k_a, r_k = vec_ref[0:1, :], vec_ref[1:2, :], vec_ref[2:3, :], vec_ref[3:4, :]
    a = _sigmoid(a0 + _mm(ad, a2_ref[...]))
    g = _mm(_sigmoid(gd), g2_ref[...])
    kx = k * k_k
    kk = kx * lax.rsqrt(_head_sum(kx * kx, RW_HEAD) + 1e-6)
    k2 = k * (1.0 + (a - 1.0) * k_a)
    bonus = _head_sum(r * k2 * r_k, RW_HEAD) * v
    cols = [r, k2, v, -kk, kk * a]
    for d in range(2):
        wd_d = pm[:, base + d * RW_DECAY_LORA:base + (d + 1) * RW_DECAY_LORA]
        pre = vec_ref[4 + d:5 + d, :] + _mm(jnp.tanh(wd_d), w2_ref[d])
        cols.append(-jnp.exp(-_softplus(-pre) - 0.5))
    cols += [bonus, g]
    for j, cval in enumerate(cols):
        o_ref[:, j * RW_DIM:(j + 1) * RW_DIM] = cval


def _rw_prep_kernel(tiles_per_batch, n_lat_tiles, x_ref, u_ref, d_ref, mu_ref, vec_ref, a2_ref, g2_ref, w2_ref, o_ref):
    i = pl.program_id(0)
    x = x_ref[...]
    rows = x.shape[0]
    ridx = lax.broadcasted_iota(jnp.int32, x.shape, 0)
    lane = lax.broadcasted_iota(jnp.int32, x.shape, 1)
    before = _shift_rows(x, 1)
    after = _shift_rows(x, -1)
    col = ridx % GRID_W
    left = jnp.where(col == 0, 0.0, before)
    right = jnp.where(col == GRID_W - 1, 0.0, after)
    top = jnp.where(i % tiles_per_batch == 0, 0.0, u_ref[...])
    bot = jnp.where(i % tiles_per_batch == tiles_per_batch - 1, 0.0, d_ref[...])
    up = jnp.concatenate([top, x[:rows - GRID_W, :]], axis=0)
    down = jnp.concatenate([x[GRID_W:, :], bot], axis=0)
    sel = lane % 4
    shifted_lat = jnp.where(sel == 0, left, jnp.where(sel == 1, right, jnp.where(sel == 2, up, down)))
    prev = jnp.where(ridx == 0, 0.0, before)
    nxt = jnp.where(ridx == rows - 1, 0.0, after)
    shifted_ctx = jnp.where(lane % 2 == 0, prev, nxt)
    shifted = jnp.where(i < n_lat_tiles, shifted_lat, shifted_ctx)
    _rw_post(x, shifted, mu_ref, vec_ref, a2_ref, g2_ref, w2_ref, o_ref)


def _rw_prep(p_rw, mu, vec, a2, g2, w2, n_batch, seq, ctx_len):
    ntok, wd = p_rw.shape
    n_lat = n_batch * seq
    wo = RW_NOUT * RW_DIM
    tm = RW_TM
    assert ctx_len == tm and seq % tm == 0 and tm % GRID_W == 0
    per = tm // GRID_W
    last_blk = ntok // GRID_W - 1
    consts = [pl.BlockSpec(a.shape, (lambda i, nd=a.ndim: (0,) * nd)) for a in (mu, vec, a2, g2, w2)]
    return pl.pallas_call(
        functools.partial(_rw_prep_kernel, seq // tm, n_lat // tm),
        grid=(ntok // tm,),
        in_specs=[pl.BlockSpec((tm, wd), lambda i: (i, 0)),
                  pl.BlockSpec((GRID_W, wd), lambda i: (jnp.maximum(i * per - 1, 0), 0)),
                  pl.BlockSpec((GRID_W, wd), lambda i: (jnp.minimum((i + 1) * per, last_blk), 0))] + consts,
        out_specs=pl.BlockSpec((tm, wo), lambda i: (i, 0)),
        out_shape=jax.ShapeDtypeStruct((ntok, wo), F32),
        compiler_params=_cparams(1),
        name="rw_prep",
    )(p_rw, p_rw, p_rw, mu, vec, a2, g2, w2)


def _rw_scan_kernel(rf, kf, vf, af, bf, wf, rb, kb, vb, ab, bb, wb, yf_ref, yb_ref, st_ref):
    c = RW_CHUNK

    @pl.when(pl.program_id(1) == 0)
    def _():
        st_ref[...] = jnp.zeros_like(st_ref)

    ri, ci = _tri_iota(c)
    nr = RW_HEADS * c
    head_mask = (lax.broadcasted_iota(jnp.int32, (nr, RW_DIM), 0) // c
                 == lax.broadcasted_iota(jnp.int32, (nr, RW_DIM), 1) // RW_HEAD)

    def stacked(x):
        xb = x.astype(BF16)
        return jnp.where(head_mask, jnp.concatenate([xb] * RW_HEADS, axis=0), jnp.zeros((), BF16))

    for d, (refs, y_ref) in enumerate((((rf, kf, vf, af, bf, wf), yf_ref), ((rb, kb, vb, ab, bb, wb), yb_ref))):
        incl_c = (ri >= ci) if d == 0 else (ri <= ci)
        incl, strict = _block_tri(RW_HEADS, c, d == 1)
        last = c - 1 if d == 0 else 0
        pre = []
        for s in range(SCAN_SUB):
            r, k, v, a, b, lw = [ref[s * c:(s + 1) * c, :] for ref in refs]
            cum = _mm_exact_lhs(incl_c.astype(F32), lw)
            lc = cum[last:last + 1, :]
            e_neg = jnp.exp(-cum)
            e_rel = jnp.exp(lc - cum)
            rt, at = stacked(r * jnp.exp(cum)), stacked(a * jnp.exp(cum - lw))
            bt, kt = stacked(b * e_neg), stacked(k * e_neg)
            bl, kl = stacked(b * e_rel), stacked(k * e_rel)
            vs = stacked(v)
            aab = jnp.where(strict, _mm_nt(at, bt), 0.0)
            aak = jnp.where(strict, _mm_nt(at, kt), 0.0)
            arb = jnp.where(incl, _mm_nt(rt, bt), 0.0)
            ark = jnp.where(incl, _mm_nt(rt, kt), 0.0)
            tinv = _neumann(aab, c)
            pre.append((at, rt, vs, bl, kl, _mm(aak, vs), _mm(ark, vs), arb, tinv, jnp.exp(lc)))
        state = st_ref[d]
        for s in (range(SCAN_SUB) if d == 0 else reversed(range(SCAN_SUB))):
            at, rt, vs, bl, kl, aak_v, ark_v, arb, tinv, e_last = pre[s]
            u = _mm(tinv, _mm_nt(at, state) + aak_v)
            y = _mm_nt(rt, state) + _mm(arb, u) + ark_v
            state = state * e_last + _mm_tn(u, bl) + _mm_tn(vs, kl)
            y_ref[s * c:(s + 1) * c, :] = functools.reduce(
                lambda p, q: p + q, [y[h * c:(h + 1) * c] for h in range(RW_HEADS)])
        st_ref[d] = state


def _rw_scan(rwp, n_batch, seq, ctx_len):
    ntok = rwp.shape[0]
    c = RW_CHUNK * SCAN_SUB
    fwd, bwd, steps = _chunk_maps(n_batch * seq, seq, ctx_len, c)

    def specs(m, wcol):
        return [pl.BlockSpec((c, RW_DIM), lambda b, t, j=j: (m(b, t), j)) for j in (0, 1, 2, 3, 4, wcol)]

    return pl.pallas_call(
        _rw_scan_kernel,
        grid=(n_batch, steps),
        in_specs=specs(fwd, 5) + specs(bwd, 6),
        out_specs=[pl.BlockSpec((c, RW_DIM), lambda b, t: (fwd(b, t), 0)),
                   pl.BlockSpec((c, RW_DIM), lambda b, t: (bwd(b, t), 0))],
        out_shape=[jax.ShapeDtypeStruct((ntok, RW_DIM), F32)] * 2,
        scratch_shapes=[pltpu.VMEM((2, RW_DIM, RW_DIM), F32)],
        compiler_params=_cparams(2),
        name="rw_scan",
    )(*([rwp] * 12))


def _outproj_kernel(n_lat_tiles, x_ref, ymf, ymb, xs_ref, z_ref, odf, odb, ocf, ocb, gate_ref, yrf, yrb, bonus_ref,
                    g_ref, vec_ref, dnw_ref, wout_ref, gmsa_ref, n2w_ref, sh_ref, sc_ref, wr_ref, br_ref,
                    xo_ref, h2_ref, route_ref, cnt_ref, run_ref):
    tm = x_ref.shape[0]

    @pl.when(pl.program_id(0) == 0)
    def _():
        run_ref[...] = jnp.zeros_like(run_ref)

    ym = (xs_ref[...] * vec_ref[0:1, :] + ymf[...] + ymb[...]) * _silu(z_ref[...])
    gw = M_INNER // M_GROUPS
    parts = []
    for g in range(M_GROUPS):
        yg = ym[:, g * gw:(g + 1) * gw]
        parts.append(yg * lax.rsqrt(jnp.mean(yg * yg, axis=-1, keepdims=True) + 1e-5))
    mix_m = jnp.concatenate(parts, axis=1) * vec_ref[1:2, :]
    od = jnp.where(pl.program_id(0) < n_lat_tiles, odf[...] + odb[...], ocf[...] + ocb[...])
    parts = []
    for h in range(DN_HEADS):
        oh = od[:, h * DN_HEAD_V:(h + 1) * DN_HEAD_V]
        parts.append(oh * lax.rsqrt(jnp.mean(oh * oh, axis=-1, keepdims=True) + 1e-6))
    mix_d = jnp.concatenate(parts, axis=1) * dnw_ref[...] * _silu(gate_ref[...])
    yr = yrf[...] + yrb[...]
    mean = _head_sum(yr, RW_HEAD) * (1.0 / RW_HEAD)
    cen = yr - mean
    var = _head_sum(cen * cen, RW_HEAD) * (1.0 / RW_HEAD)
    mix_r = ((cen * lax.rsqrt(var + RW_LN_EPS)) * vec_ref[2:3, :] + vec_ref[3:4, :] + bonus_ref[...]) * g_ref[...]

    attn = (_mm(mix_m, wout_ref[0:M_INNER, :]) + _mm(mix_d, wout_ref[M_INNER:M_INNER + DN_VD, :])
            + _mm(mix_r, wout_ref[M_INNER + DN_VD:, :]))
    xn = x_ref[...] + gmsa_ref[...] * attn
    xo_ref[...] = xn
    h2 = xn * lax.rsqrt(jnp.mean(xn * xn, axis=-1, keepdims=True) + NORM_EPS) * n2w_ref[...]
    h2 = h2 * (1.0 + sc_ref[...]) + sh_ref[...]
    h2_ref[...] = h2

    logits = _mm3(h2, wr_ref[...]) + br_ref[...]
    lane = lax.broadcasted_iota(jnp.int32, (tm, LANES), 1)
    neg = jnp.float32(-jnp.inf)
    lg = jnp.where(lane < N_EXPERTS, logits, neg)
    tops, idxs = [], []
    onehot = jnp.zeros((tm, LANES), F32)
    for _ in range(TOP_K):
        m = jnp.max(lg, axis=-1, keepdims=True)
        idx = jnp.min(jnp.where(lg == m, lane, LANES), axis=-1, keepdims=True)
        sel = lane == idx
        tops.append(m)
        idxs.append(idx)
        lg = jnp.where(sel, neg, lg)
        onehot = onehot + sel.astype(F32)
    ri, ci = _tri_iota(tm)
    before = jnp.dot((ri > ci).astype(BF16), onehot.astype(BF16), preferred_element_type=F32) + run_ref[0:1, :]
    run_ref[0:1, :] = run_ref[0:1, :] + jnp.sum(onehot, axis=0, keepdims=True)
    exps = [jnp.exp(m - tops[0]) for m in tops]
    denom = exps[0] + exps[1] + exps[2] + exps[3]
    route = jnp.zeros((tm, LANES), F32)
    for k in range(TOP_K):
        rank = jnp.sum(jnp.where(lane == idxs[k], before, 0.0), axis=-1, keepdims=True)
        route = jnp.where(lane == k, idxs[k].astype(F32), route)
        route = jnp.where(lane == TOP_K + k, exps[k] / denom, route)
        route = jnp.where(lane == 2 * TOP_K + k, rank, route)
    route_ref[...] = route
    cnt_ref[...] = run_ref[...]


def _outproj(x_all, ssd_y, xbc, p_m, dn_o, p_gate, rw_y, rwp, vec, dnw, w_out, mod3, layer, norm2_w,
             w_router, b_router, n_rows, seq, n_batch):
    d = x_all.shape[1]
    tm = OUT_TM
    row = lambda wd, col=0: pl.BlockSpec((tm, wd), lambda i, col=col: (i, col))
    const = lambda a: pl.BlockSpec(a.shape, lambda i: (0,) * a.ndim)
    n2 = norm2_w.reshape(1, d)
    n_lat_tiles = n_batch * seq // tm
    lat_row = pl.BlockSpec((tm, DN_VD), lambda i: (jnp.minimum(i, n_lat_tiles - 1), 0))
    ctx_row = pl.BlockSpec((tm, DN_VD), lambda i: (jnp.maximum(i - n_lat_tiles, 0), 0))
    ins = [x_all, ssd_y[0], ssd_y[1], xbc, p_m, dn_o[0], dn_o[1], dn_o[2], dn_o[3], p_gate, rw_y[0], rw_y[1],
           rwp, rwp, vec, dnw, w_out, mod3, n2, mod3, mod3, w_router, b_router]
    specs = [row(d), row(M_INNER), row(M_INNER), row(M_INNER, 0), row(M_INNER, M_CONV_DIM // M_INNER),
             lat_row, lat_row, ctx_row, ctx_row, row(DN_VD), row(RW_DIM), row(RW_DIM), row(RW_DIM, 7),
             row(RW_DIM, 8), const(vec), const(dnw), const(w_out), _mod_spec(layer, 2, tm, seq, n_batch),
             const(n2), _mod_spec(layer, 3, tm, seq, n_batch), _mod_spec(layer, 4, tm, seq, n_batch),
             const(w_router), const(b_router)]
    return pl.pallas_call(
        functools.partial(_outproj_kernel, n_lat_tiles),
        grid=(n_rows // tm,),
        in_specs=specs,
        out_specs=[row(d), row(d), row(LANES), pl.BlockSpec((8, LANES), lambda i: (0, 0))],
        out_shape=[jax.ShapeDtypeStruct((n_rows, d), F32), jax.ShapeDtypeStruct((n_rows, d), F32),
                   jax.ShapeDtypeStruct((n_rows, LANES), F32), jax.ShapeDtypeStruct((8, LANES), F32)],
        scratch_shapes=[pltpu.VMEM((8, LANES), F32)],
        compiler_params=_cparams(1),
        name="finish_outproj_router",
    )(*ins)


def _dispatch_kernel(dest_ref, h_ref, zeros_ref, xs_ref, sem):
    del zeros_ref
    tm = h_ref.shape[0]

    def copy(i, k):
        return pltpu.make_async_copy(h_ref.at[pl.ds(i, 1)], xs_ref.at[pl.ds(dest_ref[0, i * TOP_K + k], 1)], sem)

    def start(i, carry):
        for k in range(TOP_K):
            copy(i, k).start(priority=k % 2)
        return carry

    def wait(i, carry):
        for k in range(TOP_K):
            copy(i, k).wait()
        return carry

    lax.fori_loop(0, tm, start, 0)
    lax.fori_loop(0, tm, wait, 0)


def _dispatch(h2, dest, n_slots):
    n, d = h2.shape
    tm = MOE_TM
    return pl.pallas_call(
        _dispatch_kernel,
        grid=(n // tm,),
        in_specs=[pl.BlockSpec((None, 1, tm * TOP_K), lambda i: (i, 0, 0), memory_space=pltpu.SMEM),
                  pl.BlockSpec((tm, d), lambda i: (i, 0)),
                  pl.BlockSpec(memory_space=pl.ANY)],
        out_specs=pl.BlockSpec(memory_space=pl.ANY),
        out_shape=jax.ShapeDtypeStruct((n_slots, d), F32),
        scratch_shapes=[pltpu.SemaphoreType.DMA(())],
        input_output_aliases={2: 0},
        compiler_params=_cparams(1),
        name="moe_dispatch",
    )(dest.reshape(n // tm, 1, tm * TOP_K), h2, jnp.zeros((n_slots, d), F32))


def _moe_kernel(be_ref, nv_ref, x_ref, wgu_ref, bgu_ref, wdn_ref, bdn_ref, y_ref, wgu_bf, wdn_bf):
    j = pl.program_id(0)

    @pl.when(j < nv_ref[0])
    def _():
        changed = jnp.logical_or(j == 0, be_ref[j] != be_ref[jnp.maximum(j - 1, 0)])

        @pl.when(changed)
        def _():
            wgu_bf[...] = wgu_ref[...].astype(BF16)
            wdn_bf[...] = wdn_ref[...].astype(BF16)

        gu = jnp.dot(x_ref[...].astype(BF16), wgu_bf[...], preferred_element_type=F32) + bgu_ref[...]
        g_ = jnp.minimum(gu[:, :D_FF], SWIGLU_LIMIT)
        u_ = jnp.clip(gu[:, D_FF:], -SWIGLU_LIMIT, SWIGLU_LIMIT)
        act = g_ * _sigmoid(SWIGLU_ALPHA * g_) * (u_ + 1.0)
        y_ref[...] = jnp.dot(act.astype(BF16), wdn_bf[...], preferred_element_type=F32) + bdn_ref[...]

    @pl.when(j >= nv_ref[0])
    def _():
        y_ref[...] = jnp.zeros_like(y_ref)


def _moe(xs, block_expert, n_valid, layer, w_gate_up, b_gate_up, w_down, b_down):
    n_slots, d = xs.shape
    bm = MOE_BLOCK
    n_layers, n_exp, _, f2 = w_gate_up.shape
    blk = lambda j, be, nv: (jnp.minimum(j, nv[0] - 1), 0)
    grid_spec = pltpu.PrefetchScalarGridSpec(
        num_scalar_prefetch=2,
        grid=(n_slots // bm,),
        in_specs=[pl.BlockSpec((bm, d), blk),
                  pl.BlockSpec((None, None, d, f2), lambda j, be, nv: (layer, be[j], 0, 0)),
                  pl.BlockSpec((None, None, 1, f2), lambda j, be, nv: (layer, be[j], 0, 0)),
                  pl.BlockSpec((None, None, f2 // 2, d), lambda j, be, nv: (layer, be[j], 0, 0)),
                  pl.BlockSpec((None, None, 1, d), lambda j, be, nv: (layer, be[j], 0, 0))],
        out_specs=pl.BlockSpec((bm, d), lambda j, be, nv: (j, 0)),
        scratch_shapes=[pltpu.VMEM((d, f2), BF16), pltpu.VMEM((f2 // 2, d), BF16)],
    )
    return pl.pallas_call(
        _moe_kernel,
        grid_spec=grid_spec,
        out_shape=jax.ShapeDtypeStruct((n_slots, d), F32),
        compiler_params=_cparams(1),
        name="moe_experts",
    )(block_expert, n_valid, xs, w_gate_up, b_gate_up.reshape(n_layers, n_exp, 1, f2), w_down,
      b_down.reshape(n_layers, n_exp, 1, d))


def _combine_kernel(final_norm, dest_ref, x_ref, route_ref, g_ref, nf_ref, y_ref, o_ref, buf, sem):
    tm = x_ref.shape[0]

    def copy(i, k):
        return pltpu.make_async_copy(y_ref.at[pl.ds(dest_ref[0, i * TOP_K + k], 1)], buf.at[k, pl.ds(i, 1)], sem)

    def start(i, carry):
        for k in range(TOP_K):
            copy(i, k).start(priority=k % 2)
        return carry

    def wait(i, carry):
        for k in range(TOP_K):
            copy(i, k).wait()
        return carry

    lax.fori_loop(0, tm, start, 0)
    lax.fori_loop(0, tm, wait, 0)
    route = route_ref[...]
    acc = jnp.zeros(x_ref.shape, F32)
    for k in range(TOP_K):
        acc = acc + route[:, TOP_K + k:TOP_K + k + 1] * buf[k]
    out = x_ref[...] + g_ref[...] * acc
    if final_norm:
        out = out * lax.rsqrt(jnp.mean(out * out, axis=-1, keepdims=True) + NORM_EPS) * nf_ref[...]
    o_ref[...] = out


def _combine(x_new, route, dest, y_slots, mod3, layer, norm_f_w, final_norm, seq, n_batch):
    n, d = x_new.shape
    tm = MOE_TM
    row = lambda wd: pl.BlockSpec((tm, wd), lambda i: (i, 0))
    return pl.pallas_call(
        functools.partial(_combine_kernel, final_norm),
        grid=(n // tm,),
        in_specs=[pl.BlockSpec((None, 1, tm * TOP_K), lambda i: (i, 0, 0), memory_space=pltpu.SMEM),
                  row(d), row(LANES), _mod_spec(layer, 5, tm, seq, n_batch),
                  pl.BlockSpec((1, d), lambda i: (0, 0)),
                  pl.BlockSpec(memory_space=pl.ANY)],
        out_specs=row(d),
        out_shape=jax.ShapeDtypeStruct((n, d), F32),
        scratch_shapes=[pltpu.VMEM((TOP_K, tm, d), F32), pltpu.SemaphoreType.DMA(())],
        compiler_params=_cparams(1),
        name="moe_combine",
    )(dest.reshape(n // tm, 1, tm * TOP_K), x_new, route, mod3, norm_f_w.reshape(1, d), y_slots)


def _routing_tables(route, counts, n_slots):
    bm = MOE_BLOCK
    top_e = route[:, :TOP_K].astype(jnp.int32)
    rank = route[:, 2 * TOP_K:3 * TOP_K].astype(jnp.int32)
    cnt = counts[0, :N_EXPERTS].astype(jnp.int32)
    padded = (cnt + bm - 1) // bm * bm
    ends = jnp.cumsum(padded)
    dest = (ends - padded)[top_e] + rank
    n_blocks = n_slots // bm
    n_valid = ends[-1] // bm
    blk = jnp.minimum(jnp.arange(n_blocks, dtype=jnp.int32), n_valid - 1)
    block_expert = jnp.sum((blk[:, None] >= (ends // bm)[None, :]).astype(jnp.int32), axis=1)
    block_expert = jnp.minimum(block_expert, N_EXPERTS - 1)
    return dest.reshape(-1), block_expert, n_valid.reshape(1).astype(jnp.int32)


def _pad_cols(a, width):
    return jnp.pad(a, [(0, 0)] * (a.ndim - 1) + [(0, width - a.shape[-1])])


def kernel(x, c, ctx, c_ctx, w_mod, b_mod, norm1_w, norm2_w, w_in, w_out, m_conv_w, m_conv_b, m_dt_bias, m_a_log, m_d, m_norm_w, dn_conv_w, dn_dt_bias, dn_a_log, dn_norm_w, rw_mu, rw_w0, rw_w2, rw_a0, rw_a2, rw_g2, rw_k_k, rw_k_a, rw_r_k, rw_ln_w, rw_ln_b, w_router, b_router, w_gate_up, b_gate_up, w_down, b_down, norm_f_w):
    n_batch, seq, d = x.shape
    ctx_len = ctx.shape[1]
    depth = w_mod.shape[0]
    n_lat, n_ctx = n_batch * seq, n_batch * ctx_len
    assert d == D_MODEL and seq // GRID_W == DN_CHUNK and n_batch + 1 <= 8

    x_all = jnp.concatenate([x.reshape(n_lat, d), ctx.reshape(n_ctx, d)], axis=0)
    c8 = jnp.zeros((8, d), F32).at[:n_batch].set(c).at[n_batch].set(c_ctx)
    mod3 = _modulation(c8, w_mod, b_mod).reshape(depth * 6 * 8, 1, d)

    out = None
    for i in range(depth):
        last = i == depth - 1
        wi = w_in[i]
        mo, do, ro = 0, M_COLS, M_COLS + DN_COLS
        w_m = jnp.concatenate([wi[:, mo + M_INNER:mo + M_INNER + M_CONV_DIM], wi[:, mo:mo + M_INNER]], axis=1)
        w_q = wi[:, do:do + DN_KD]
        w_k = wi[:, do + DN_KD:do + 2 * DN_KD]
        w_v = wi[:, do + 2 * DN_KD:do + DN_CONV_DIM]
        w_g = wi[:, do + DN_CONV_DIM:do + DN_CONV_DIM + DN_VD]
        w_r = _pad_cols(wi[:, ro:ro + RW_COLS], RW_COLS_PAD)
        w_s = _pad_cols(jnp.concatenate([wi[:, mo + M_INNER + M_CONV_DIM:mo + M_COLS],
                                         wi[:, do + DN_CONV_DIM + DN_VD:do + DN_COLS]], axis=1), LANES)
        weights = [w.astype(BF16) for w in (w_m, w_q, w_k, w_v, w_g, w_r, w_s)]
        p_m, p_q, p_k, p_v, p_gate, p_rw, small = _inproj(x_all, norm1_w[i], mod3, i, weights, seq, n_batch)

        xbc = _ssd_prep(p_m, m_conv_w[i], m_conv_b[i], seq, n_lat, ctx_len)
        ssd_par = jnp.zeros((8, LANES), F32)
        ssd_par = ssd_par.at[0, :2 * M_HEADS].set(m_dt_bias[i].reshape(-1))
        ssd_par = ssd_par.at[1, :2 * M_HEADS].set(-jnp.exp(m_a_log[i].astype(F32)).reshape(-1))
        ssd_y = _ssd_scan(xbc, small, ssd_par, n_batch, seq, ctx_len)

        lat_qkv, ctx_qkv = _dn_prep(p_q, p_k, p_v, dn_conv_w[i], n_batch, seq, ctx_len)
        dn_par = jnp.zeros((8, LANES), F32)
        dn_par = dn_par.at[0, 16:16 + 2 * DN_HEADS].set(dn_dt_bias[i].reshape(-1))
        dn_par = dn_par.at[1, 16:16 + 2 * DN_HEADS].set(-jnp.exp(dn_a_log[i].astype(F32)).reshape(-1))
        dn_o = _dn_scan(lat_qkv, ctx_qkv, small, dn_par, n_batch, seq, ctx_len)

        rw_vec = jnp.stack([rw_a0[i], rw_k_k[i], rw_k_a[i], rw_r_k[i].reshape(-1), rw_w0[i, 0], rw_w0[i, 1],
                            jnp.zeros((RW_DIM,), F32), jnp.zeros((RW_DIM,), F32)])
        rwp = _rw_prep(p_rw, _pad_cols(rw_mu[i], RW_COLS_PAD).reshape(1, -1), rw_vec, rw_a2[i], rw_g2[i], rw_w2[i],
                       n_batch, seq, ctx_len)
        rw_y = _rw_scan(rwp, n_batch, seq, ctx_len)

        n_rows = n_lat if last else n_lat + n_ctx
        fin_vec = jnp.stack([jnp.repeat(m_d[i], M_HEAD_DIM), m_norm_w[i], rw_ln_w[i], rw_ln_b[i]]
                            + [jnp.zeros((RW_DIM,), F32)] * 4)
        dnw = jnp.tile(dn_norm_w[i], DN_HEADS).reshape(1, DN_VD)
        x_new, h2, route, counts = _outproj(
            x_all, ssd_y, xbc, p_m, dn_o, p_gate, rw_y, rwp, fin_vec, dnw, w_out[i].astype(BF16), mod3, i,
            norm2_w[i], _pad_cols(w_router[i], LANES), _pad_cols(b_router[i].reshape(1, -1), LANES),
            n_rows, seq, n_batch)

        n_slots = (n_rows * TOP_K // MOE_BLOCK + N_EXPERTS) * MOE_BLOCK
        dest, block_expert, n_valid = _routing_tables(route, counts, n_slots)
        xs = _dispatch(h2, dest, n_slots)
        y_slots = _moe(xs, block_expert, n_valid, i, w_gate_up, b_gate_up, w_down, b_down)
        res = _combine(x_new, route, dest, y_slots, mod3, i, norm_f_w, last, seq, n_batch)
        if last:
            out = res
        else:
            x_all = res
    return out.reshape(n_batch, seq, d)
```

```python
import functools
import math

import jax
import jax.numpy as jnp
from jax import lax
from jax.experimental import pallas as pl
from jax.experimental.pallas import tpu as pltpu

F32 = jnp.float32
BF16 = jnp.bfloat16

D_MODEL = 1024
GRID_W = 64
NORM_EPS = 1e-6

M_HEADS, M_HEAD_DIM, M_STATE, M_GROUPS = 4, 64, 128, 2
M_INNER = M_HEADS * M_HEAD_DIM
M_CONV_DIM = M_INNER + 2 * M_GROUPS * M_STATE
M_COLS = M_INNER + M_CONV_DIM + 2 * M_HEADS
M_CHUNK = 128

DN_HEADS, DN_HEAD_K, DN_HEAD_V = 4, 128, 128
DN_KD = DN_HEADS * DN_HEAD_K
DN_VD = DN_HEADS * DN_HEAD_V
DN_CONV_DIM = 2 * DN_KD + DN_VD
DN_COLS = DN_CONV_DIM + DN_VD + 4 * DN_HEADS
DN_CHUNK = 64

RW_HEADS, RW_HEAD = 4, 64
RW_DIM = RW_HEADS * RW_HEAD
RW_DECAY_LORA, RW_ICLR_LORA, RW_GATE_LORA = 64, 64, 128
RW_COLS = 3 * RW_DIM + 2 * RW_DECAY_LORA + RW_ICLR_LORA + RW_GATE_LORA
RW_COLS_PAD = 1152
RW_LN_EPS = RW_HEAD * 1e-5
RW_CHUNK = 64
RW_NOUT = 9

N_EXPERTS, TOP_K = 32, 4
D_FF = 1024
SWIGLU_ALPHA, SWIGLU_LIMIT = 1.702, 7.0
MOE_BLOCK = 512

LANES = 128
VMEM_LIMIT = 56 * 1024 * 1024

IN_TM = 512
CONV_TM = 256
DN_G = 8
RW_TM = 256
OUT_TM = 256
MOE_TM = 256
SCAN_SUB = 2


def _cparams(n_axes):
    return pltpu.CompilerParams(dimension_semantics=("arbitrary",) * n_axes, vmem_limit_bytes=VMEM_LIMIT)


def _sigmoid(x):
    return 1.0 / (1.0 + jnp.exp(-x))


def _silu(x):
    return x * _sigmoid(x)


def _softplus(x):
    return jnp.maximum(x, 0.0) + jnp.log(1.0 + jnp.exp(-jnp.abs(x)))


def _mm(a, b):
    return jnp.dot(a.astype(BF16), b.astype(BF16), preferred_element_type=F32)


def _mm_nt(a, b):
    return lax.dot_general(a.astype(BF16), b.astype(BF16), (((1,), (1,)), ((), ())), preferred_element_type=F32)


def _mm_tn(a, b):
    return lax.dot_general(a.astype(BF16), b.astype(BF16), (((0,), (0,)), ((), ())), preferred_element_type=F32)


def _split3(x):
    hi = x.astype(BF16)
    r1 = x - hi.astype(F32)
    mid = r1.astype(BF16)
    lo = (r1 - mid.astype(F32)).astype(BF16)
    return hi, mid, lo


def _mm3(a, b):
    ah, am, _ = _split3(a)
    bh, bm, _ = _split3(b)
    dot = functools.partial(jnp.dot, preferred_element_type=F32)
    return dot(ah, bh) + (dot(am, bh) + dot(ah, bm))


def _mm_exact_lhs(a01, x):
    a = a01.astype(BF16)
    dot = functools.partial(jnp.dot, preferred_element_type=F32)
    hi, mid, lo = _split3(x)
    return dot(a, hi) + (dot(a, mid) + dot(a, lo))


def _mm_exact_rhs(x, b01):
    b = b01.astype(BF16)
    dot = functools.partial(jnp.dot, preferred_element_type=F32)
    hi, mid, lo = _split3(x)
    return dot(hi, b) + (dot(mid, b) + dot(lo, b))


def _to_rows(x, n_rows):
    sel = (lax.broadcasted_iota(jnp.int32, (n_rows, LANES), 0)
           == lax.broadcasted_iota(jnp.int32, (n_rows, LANES), 1)).astype(BF16)
    nt = functools.partial(lax.dot_general, dimension_numbers=(((1,), (1,)), ((), ())), preferred_element_type=F32)
    hi, mid, lo = _split3(x)
    return nt(sel, hi) + (nt(sel, mid) + nt(sel, lo))


def _tri_iota(c):
    return lax.broadcasted_iota(jnp.int32, (c, c), 0), lax.broadcasted_iota(jnp.int32, (c, c), 1)


def _neumann_many(ms, nil):
    ri, ci = _tri_iota(ms[0].shape[0])
    eye = (ri == ci).astype(F32)
    sums = [eye + m for m in ms]
    pows = list(ms)
    for _ in range(int(math.log2(nil)) - 1):
        pows = [_mm(p, p) for p in pows]
        sums = [s + _mm(s, p) for s, p in zip(sums, pows)]
    return sums


def _stack_rows(x, n, width):
    return jnp.concatenate([x[:, h * width:(h + 1) * width] for h in range(n)], axis=0)


def _block_tri(n_blocks, c, reverse):
    ri, ci = _tri_iota(n_blocks * c)
    same = (ri // c) == (ci // c)
    if reverse:
        return jnp.logical_and(same, ri <= ci), jnp.logical_and(same, ri < ci)
    return jnp.logical_and(same, ri >= ci), jnp.logical_and(same, ri > ci)


def _shift_rows(x, n):
    return pltpu.roll(x, n % x.shape[0], 0)


def _conv3(x, prev_row, next_row, w):
    rows = x.shape[0]
    ridx = lax.broadcasted_iota(jnp.int32, x.shape, 0)
    up = jnp.where(ridx == 0, prev_row, _shift_rows(x, 1))
    dn = jnp.where(ridx == rows - 1, next_row, _shift_rows(x, -1))
    return up * w[0:1, :] + x * w[1:2, :] + dn * w[2:3, :]


def _head_sum(x, width):
    n = x.shape[1]
    ri, ci = _tri_iota(n)
    return _mm_exact_rhs(x, (ri // width == ci // width).astype(F32))


def _mod_kernel(c_ref, w_ref, b_ref, o_ref):
    o_ref[...] = _mm3(_silu(c_ref[...]), w_ref[...]) + b_ref[...]


def _modulation(c8, w_mod, b_mod):
    n_layers, d = w_mod.shape[0], w_mod.shape[1]
    return pl.pallas_call(
        _mod_kernel,
        grid=(n_layers, 6),
        in_specs=[
            pl.BlockSpec((8, d), lambda l, j: (0, 0)),
            pl.BlockSpec((None, d, d), lambda l, j: (l, 0, j)),
            pl.BlockSpec((None, None, 1, d), lambda l, j: (l, j, 0, 0)),
        ],
        out_specs=pl.BlockSpec((None, None, 8, d), lambda l, j: (l, j, 0, 0)),
        out_shape=jax.ShapeDtypeStruct((n_layers, 6, 8, d), F32),
        compiler_params=_cparams(2),
        name="adaln_mod",
    )(c8, w_mod, b_mod.reshape(n_layers, 6, 1, d))


def _mod_spec(layer, which, rows_per_tile, seq, n_batch):
    base = (layer * 6 + which) * 8
    tiles_per_batch = seq // rows_per_tile

    def imap(i, *_):
        return (base + jnp.minimum(i // tiles_per_batch, n_batch), 0, 0)

    return pl.BlockSpec((None, 1, D_MODEL), imap)


def _inproj_kernel(x_ref, nw_ref, sh_ref, sc_ref, *refs):
    n = len(refs) // 2
    x = x_ref[...]
    h = x * lax.rsqrt(jnp.mean(x * x, axis=-1, keepdims=True) + NORM_EPS) * nw_ref[...]
    h = (h * (1.0 + sc_ref[...]) + sh_ref[...]).astype(BF16)
    for w_ref, o_ref in zip(refs[:n], refs[n:]):
        o_ref[...] = jnp.dot(h, w_ref[...], preferred_element_type=F32)


def _inproj(x_all, norm_w, mod3, layer, weights, seq, n_batch):
    ntok, d = x_all.shape
    widths = [w.shape[1] for w in weights]
    row = lambda wd: pl.BlockSpec((IN_TM, wd), lambda i: (i, 0))
    return pl.pallas_call(
        _inproj_kernel,
        grid=(ntok // IN_TM,),
        in_specs=[row(d), pl.BlockSpec((1, d), lambda i: (0, 0)),
                  _mod_spec(layer, 0, IN_TM, seq, n_batch), _mod_spec(layer, 1, IN_TM, seq, n_batch)]
                 + [pl.BlockSpec(w.shape, lambda i: (0, 0)) for w in weights],
        out_specs=[row(wd) for wd in widths],
        out_shape=[jax.ShapeDtypeStruct((ntok, wd), F32) for wd in widths],
        compiler_params=_cparams(1),
        name="norm_mod_inproj",
    )(x_all, norm_w.reshape(1, d), mod3, mod3, *weights)


def _chunk_maps(n_lat_rows, seq, ctx_len, chunk):
    lat_chunks, ctx_chunks = seq // chunk, ctx_len // chunk
    ctx_base = n_lat_rows // chunk

    def fwd(b, t):
        return jnp.where(t < ctx_chunks, ctx_base + b * ctx_chunks + t, b * lat_chunks + (t - ctx_chunks))

    def bwd(b, t):
        return jnp.where(t < ctx_chunks, ctx_base + b * ctx_chunks + (ctx_chunks - 1 - t),
                         b * lat_chunks + (lat_chunks - 1 - (t - ctx_chunks)))

    return fwd, bwd, lat_chunks + ctx_chunks


def _ssd_prep_kernel(tiles_per_batch, n_lat_tiles, x_ref, p_ref, n_ref, w_ref, b_ref, o_ref):
    i = pl.program_id(0)
    lat = i < n_lat_tiles
    is_start = jnp.logical_or(jnp.logical_not(lat), i % tiles_per_batch == 0)
    is_end = jnp.logical_or(jnp.logical_not(lat), i % tiles_per_batch == tiles_per_batch - 1)
    prev_row = jnp.where(is_start, 0.0, p_ref[7:8, :])
    next_row = jnp.where(is_end, 0.0, n_ref[0:1, :])
    o_ref[...] = _silu(_conv3(x_ref[...], prev_row, next_row, w_ref[...]) + b_ref[...])


def _ssd_prep(p_m, conv_w, conv_b, seq, n_lat, ctx_len):
    ntok = p_m.shape[0]
    wd = M_CONV_DIM
    assert ctx_len == CONV_TM and seq % CONV_TM == 0
    sub = CONV_TM // 8
    last8 = ntok // 8 - 1
    kern = functools.partial(_ssd_prep_kernel, seq // CONV_TM, n_lat // CONV_TM)
    return pl.pallas_call(
        kern,
        grid=(ntok // CONV_TM,),
        in_specs=[
            pl.BlockSpec((CONV_TM, wd), lambda i: (i, 0)),
            pl.BlockSpec((8, wd), lambda i: (jnp.maximum(i * sub - 1, 0), 0)),
            pl.BlockSpec((8, wd), lambda i: (jnp.minimum((i + 1) * sub, last8), 0)),
            pl.BlockSpec((3, wd), lambda i: (0, 0)),
            pl.BlockSpec((1, wd), lambda i: (0, 0)),
        ],
        out_specs=pl.BlockSpec((CONV_TM, wd), lambda i: (i, 0)),
        out_shape=jax.ShapeDtypeStruct((ntok, wd), F32),
        compiler_params=_cparams(1),
        name="ssd_conv_silu",
    )(p_m, p_m, p_m, conv_w, conv_b.reshape(1, wd))


def _ssd_scan_kernel(xf, bf, cf, sf, xb, bb, cb, sb, par_ref, yf_ref, yb_ref, st_ref):
    q = M_CHUNK

    @pl.when(pl.program_id(1) == 0)
    def _():
        st_ref[...] = jnp.zeros_like(st_ref)

    ri, ci = _tri_iota(q)
    for d, (x_ref, b_ref, c_ref, s_ref, y_ref) in enumerate(((xf, bf, cf, sf, yf_ref), (xb, bb, cb, sb, yb_ref))):
        mask = (ri >= ci) if d == 0 else (ri <= ci)
        last = q - 1 if d == 0 else 0
        dt = _softplus(s_ref[...] + par_ref[0:1, :])
        la = dt * par_ref[1:2, :]
        cum = _mm_exact_lhs(mask.astype(F32), la)
        cum_rows = _to_rows(cum, 8)
        xs, bm, cm = x_ref[...], b_ref[...], c_ref[...]
        gmat = [_mm_nt(cm[:, g * M_STATE:(g + 1) * M_STATE], bm[:, g * M_STATE:(g + 1) * M_STATE])
                for g in range(M_GROUPS)]
        outs = []
        for h in range(M_HEADS):
            c = d * M_HEADS + h
            g = h // (M_HEADS // M_GROUPS)
            cc = cum[:, c:c + 1]
            cr = cum_rows[c:c + 1, :]
            seg = jnp.where(mask, jnp.exp(jnp.minimum(cc - cr, 0.0)), 0.0)
            xin = xs[:, h * M_HEAD_DIM:(h + 1) * M_HEAD_DIM] * dt[:, c:c + 1]
            bm_g = bm[:, g * M_STATE:(g + 1) * M_STATE]
            cm_g = cm[:, g * M_STATE:(g + 1) * M_STATE]
            state = st_ref[d, h]
            outs.append(_mm(gmat[g] * seg, xin) + _mm(cm_g, state) * jnp.exp(cc))
            lc = cum[last:last + 1, c:c + 1]
            st_ref[d, h] = jnp.exp(lc) * state + _mm_tn(bm_g * jnp.exp(lc - cc), xin)
        y_ref[...] = jnp.concatenate(outs, axis=1)


def _ssd_scan(xbc, small, par, n_batch, seq, ctx_len):
    ntok = xbc.shape[0]
    q = M_CHUNK
    fwd, bwd, steps = _chunk_maps(n_batch * seq, seq, ctx_len, q)

    def specs(m):
        return [pl.BlockSpec((q, M_INNER), lambda b, t: (m(b, t), 0)),
                pl.BlockSpec((q, M_INNER), lambda b, t: (m(b, t), 1)),
                pl.BlockSpec((q, M_INNER), lambda b, t: (m(b, t), 2)),
                pl.BlockSpec((q, LANES), lambda b, t: (m(b, t), 0))]

    return pl.pallas_call(
        _ssd_scan_kernel,
        grid=(n_batch, steps),
        in_specs=specs(fwd) + specs(bwd) + [pl.BlockSpec((8, LANES), lambda b, t: (0, 0))],
        out_specs=[pl.BlockSpec((q, M_INNER), lambda b, t: (fwd(b, t), 0)),
                   pl.BlockSpec((q, M_INNER), lambda b, t: (bwd(b, t), 0))],
        out_shape=[jax.ShapeDtypeStruct((ntok, M_INNER), F32)] * 2,
        scratch_shapes=[pltpu.VMEM((2, M_HEADS, M_STATE, M_HEAD_DIM), F32)],
        compiler_params=_cparams(2),
        name="ssd_scan",
    )(xbc, xbc, xbc, small, xbc, xbc, xbc, small, par)


def _dn_norm(x, which):
    if which == 2:
        return x
    outs = []
    for h in range(DN_HEADS):
        xh = x[:, h * DN_HEAD_K:(h + 1) * DN_HEAD_K]
        n = lax.rsqrt(jnp.sum(xh * xh, axis=-1, keepdims=True) + 1e-6)
        if which == 0:
            n = n * (DN_HEAD_K ** -0.5)
        outs.append(xh * n)
    return jnp.concatenate(outs, axis=1)


def _dn_prep_ctx_kernel(q_ref, k_ref, v_ref, w_ref, oq_ref, ok_ref, ov_ref):
    for which, (i_ref, o_ref) in enumerate(((q_ref, oq_ref), (k_ref, ok_ref), (v_ref, ov_ref))):
        x = i_ref[...]
        zero = jnp.zeros((1, x.shape[1]), F32)
        w = w_ref[:, which * DN_KD:(which + 1) * DN_KD]
        o_ref[...] = _dn_norm(_silu(_conv3(x, zero, zero, w)), which)


def _dn_prep_lat_kernel(q_ref, qp_ref, qn_ref, k_ref, kp_ref, kn_ref, v_ref, vp_ref, vn_ref, w_ref,
                        oq_ref, ok_ref, ov_ref):
    g = pl.program_id(1)
    first = g == 0
    last = g == pl.num_programs(1) - 1
    wd = DN_KD
    for which, (i_ref, p_ref, n_ref, o_ref) in enumerate(((q_ref, qp_ref, qn_ref, oq_ref),
                                                         (k_ref, kp_ref, kn_ref, ok_ref),
                                                         (v_ref, vp_ref, vn_ref, ov_ref))):
        x = i_ref[...]
        rows = x.shape[0]
        ridx = lax.broadcasted_iota(jnp.int32, x.shape, 0)
        xu = _shift_rows(x, 1)
        xd = _shift_rows(x, -1)
        halo_p = jnp.where(first, 0.0, p_ref[7:8, :])
        halo_n = jnp.where(last, 0.0, n_ref[0:1, :])
        up0 = jnp.concatenate([halo_p, xu[0:1, :(DN_G - 1) * wd]], axis=1)
        dn0 = jnp.concatenate([xd[rows - 1:rows, wd:], halo_n], axis=1)
        up = jnp.where(ridx == 0, up0, xu)
        dn = jnp.where(ridx == rows - 1, dn0, xd)
        w = jnp.concatenate([w_ref[:, which * wd:(which + 1) * wd]] * DN_G, axis=1)
        y = _silu(up * w[0:1, :] + x * w[1:2, :] + dn * w[2:3, :])
        for j in range(DN_G):
            o_ref[:, j * wd:(j + 1) * wd] = _dn_norm(y[:, j * wd:(j + 1) * wd], which)


def _dn_prep(pq, pk, pv, conv_w, n_batch, seq, ctx_len):
    ntok = pq.shape[0]
    n_lat = n_batch * seq
    wd = DN_KD
    rows = seq // GRID_W
    assert rows == DN_CHUNK and GRID_W % DN_G == 0 and n_lat % ctx_len == 0
    n_ctx = ntok - n_lat
    cin = pl.BlockSpec((ctx_len, wd), lambda b: (n_lat // ctx_len + b, 0))
    cout = pl.BlockSpec((ctx_len, wd), lambda b: (b, 0))
    ctx_out = pl.pallas_call(
        _dn_prep_ctx_kernel,
        grid=(n_batch,),
        in_specs=[cin, cin, cin, pl.BlockSpec((3, 3 * wd), lambda b: (0, 0))],
        out_specs=[cout, cout, cout],
        out_shape=[jax.ShapeDtypeStruct((n_ctx, wd), F32)] * 3,
        compiler_params=_cparams(1),
        name="dn_prep_ctx",
    )(pq, pk, pv, conv_w)
    view = lambda a: a.reshape(ntok // GRID_W, GRID_W * wd)
    main = pl.BlockSpec((rows, DN_G * wd), lambda b, g: (b, g))
    prev = pl.BlockSpec((8, wd), lambda b, g: (b * (rows // 8) + rows // 8 - 1, jnp.maximum(g * DN_G - 1, 0)))
    nxt = pl.BlockSpec((8, wd), lambda b, g: (b * (rows // 8), jnp.minimum((g + 1) * DN_G, GRID_W - 1)))
    lat_out = pl.pallas_call(
        _dn_prep_lat_kernel,
        grid=(n_batch, GRID_W // DN_G),
        in_specs=[main, prev, nxt] * 3 + [pl.BlockSpec((3, 3 * wd), lambda b, g: (0, 0))],
        out_specs=[main] * 3,
        out_shape=[jax.ShapeDtypeStruct((n_lat // GRID_W, GRID_W * wd), F32)] * 3,
        compiler_params=_cparams(2),
        name="dn_prep_lat",
    )(view(pq), view(pq), view(pq), view(pk), view(pk), view(pk), view(pv), view(pv), view(pv), conv_w)
    return lat_out, ctx_out


def _dn_scan_kernel(side_by_side, qf, kf, vf, sf, qb, kb, vb, sb, par_ref, s0_ref, of_ref, ob_ref, s1_ref, st_ref):
    c = DN_CHUNK
    t = pl.program_id(1)

    @pl.when(t == 0)
    def _():
        st_ref[...] = s0_ref[...]

    def sub(ref, s, width):
        return ref[:, s * width:(s + 1) * width] if side_by_side else ref[s * c:(s + 1) * c, :]

    ri, ci = _tri_iota(c)
    dirs = ((qf, kf, vf, sf), (qb, kb, vb, sb))
    chains = [(d, s) for d in range(2) for s in range(SCAN_SUB)]
    masks = [_block_tri(DN_HEADS, c, d == 1) for d in range(2)]
    heads = range(DN_HEADS)
    ops = []
    for d, s in chains:
        q_ref, k_ref, v_ref, s_ref = dirs[d]
        incl_c = (ri >= ci) if d == 0 else (ri <= ci)
        last = c - 1 if d == 0 else 0
        lb, lg = 8 + d * DN_HEADS, 16 + d * DN_HEADS
        sm = sub(s_ref, s, LANES)
        beta_all = _sigmoid(sm)
        gl = _softplus(sm + par_ref[0:1, :]) * par_ref[1:2, :]
        cum = _mm_exact_lhs(incl_c.astype(F32), gl)
        cum_rows = _to_rows(cum, 32)
        beta = jnp.concatenate([beta_all[:, lb + h:lb + h + 1] for h in heads], axis=0)
        cc = jnp.concatenate([cum[:, lg + h:lg + h + 1] for h in heads], axis=0)
        cr = jnp.concatenate([cum_rows[lg + h:lg + h + 1, :] for h in heads], axis=1)
        decay = jnp.where(masks[d][0], jnp.exp(jnp.minimum(cc - cr, 0.0)), 0.0)
        qs = _stack_rows(sub(q_ref, s, DN_KD), DN_HEADS, DN_HEAD_K)
        ks = _stack_rows(sub(k_ref, s, DN_KD), DN_HEADS, DN_HEAD_K)
        vs = _stack_rows(sub(v_ref, s, DN_VD), DN_HEADS, DN_HEAD_V)
        kbeta = ks * beta
        ecc = jnp.exp(cc)
        lcs = [cum[last:last + 1, lg + h:lg + h + 1] for h in heads]
        ops.append(dict(decay=decay, qs=qs, ks=ks, kbeta=kbeta, qe=qs * ecc,
                        rhs=jnp.concatenate([vs * beta, kbeta * ecc], axis=1),
                        kdec=[ks[h * c:(h + 1) * c] * jnp.exp(lcs[h] - cc[h * c:(h + 1) * c]) for h in heads],
                        e_last=[jnp.exp(lc) for lc in lcs]))
    kk = [_mm_nt(o["kbeta"], o["ks"]) for o in ops]
    qk = [_mm_nt(o["qs"], o["ks"]) for o in ops]
    lower = [-jnp.where(masks[d][1], x * o["decay"], 0.0) for (d, _), x, o in zip(chains, kk, ops)]
    attn = [x * o["decay"] for x, o in zip(qk, ops)]
    tinv = _neumann_many(lower, c)
    uw = [_mm(t_, o["rhs"]) for t_, o in zip(tinv, ops)]
    states = [[st_ref[d, h] for h in heads] for d in range(2)]
    for step in range(SCAN_SUB):
        idx = [step, SCAN_SUB + (SCAN_SUB - 1 - step)]
        qw = [[_mm(jnp.concatenate([ops[i]["qe"][h * c:(h + 1) * c], uw[i][h * c:(h + 1) * c, DN_HEAD_V:]], axis=0),
                   states[d][h]) for h in heads] for d, i in enumerate(idx)]
        v_new = [[uw[i][h * c:(h + 1) * c, :DN_HEAD_V] - qw[d][h][c:] for h in heads] for d, i in enumerate(idx)]
        o = [jnp.concatenate([qw[d][h][:c] for h in heads], axis=0)
             + _mm(attn[i], jnp.concatenate(v_new[d], axis=0)) for d, i in enumerate(idx)]
        states = [[ops[i]["e_last"][h] * states[d][h] + _mm_tn(ops[i]["kdec"][h], v_new[d][h]) for h in heads]
                  for d, i in enumerate(idx)]
        for d, (i, o_ref) in enumerate(zip(idx, (of_ref, ob_ref))):
            s = chains[i][1]
            o_cols = jnp.concatenate([o[d][h * c:(h + 1) * c] for h in heads], axis=1)
            if side_by_side:
                o_ref[:, s * DN_VD:(s + 1) * DN_VD] = o_cols
            else:
                o_ref[s * c:(s + 1) * c, :] = o_cols
    for d in range(2):
        for h in heads:
            st_ref[d, h] = states[d][h]

    @pl.when(t == pl.num_programs(1) - 1)
    def _():
        s1_ref[...] = st_ref[...]


def _dn_scan(lat_qkv, ctx_qkv, small, par, n_batch, seq, ctx_len):
    ntok = small.shape[0]
    n_lat = n_batch * seq
    c = DN_CHUNK
    wd = DN_KD
    st_shape = (n_batch, 2, DN_HEADS, DN_HEAD_K, DN_HEAD_V)
    st_spec = pl.BlockSpec((None, 2, DN_HEADS, DN_HEAD_K, DN_HEAD_V), lambda b, t: (b, 0, 0, 0, 0))
    par_spec = pl.BlockSpec((8, LANES), lambda b, t: (0, 0))

    def run(arrs, sm, s0, nsteps, side_by_side, blk_of, sm_blk_of, name):
        fw = lambda f: (lambda b, t: f(b, t))
        bw = lambda f: (lambda b, t: f(b, nsteps - 1 - t))
        shp = lambda w: (c, SCAN_SUB * w) if side_by_side else (SCAN_SUB * c, w)
        ins = lambda o: [pl.BlockSpec(shp(wd), o(blk_of))] * 3 + [pl.BlockSpec(shp(LANES), o(sm_blk_of))]
        return pl.pallas_call(
            functools.partial(_dn_scan_kernel, side_by_side),
            grid=(n_batch, nsteps),
            in_specs=ins(fw) + ins(bw) + [par_spec, st_spec],
            out_specs=[pl.BlockSpec(shp(wd), fw(blk_of)), pl.BlockSpec(shp(wd), bw(blk_of)), st_spec],
            out_shape=[jax.ShapeDtypeStruct(arrs[0].shape, F32)] * 2 + [jax.ShapeDtypeStruct(st_shape, F32)],
            scratch_shapes=[pltpu.VMEM((2, DN_HEADS, DN_HEAD_K, DN_HEAD_V), F32)],
            compiler_params=_cparams(2),
            name=name,
        )(*arrs, sm, *arrs, sm, par, s0)

    ctx_steps = ctx_len // (c * SCAN_SUB)
    ctx_base = n_lat // (c * SCAN_SUB)
    ocf, ocb, s_ctx = run(ctx_qkv, small, jnp.zeros(st_shape, F32), ctx_steps, False,
                          lambda b, j: (b * ctx_steps + j, 0),
                          lambda b, j: (ctx_base + b * ctx_steps + j, 0), "dn_scan_ctx")
    olf, olb, _ = run(lat_qkv, small.reshape(ntok // GRID_W, GRID_W * LANES), s_ctx, GRID_W // SCAN_SUB, True,
                      lambda b, j: (b, j), lambda b, j: (b, j), "dn_scan_lat")
    return olf.reshape(n_lat, wd), olb.reshape(n_lat, wd), ocf, ocb


def _rw_post(x, shifted, mu_ref, vec_ref, a2_ref, g2_ref, w2_ref, o_ref):
    pm = x + (shifted - x) * mu_ref[...]
    r = pm[:, 0:RW_DIM]
    k = pm[:, RW_DIM:2 * RW_DIM]
    v = pm[:, 2 * RW_DIM:3 * RW_DIM]
    base = 3 * RW_DIM
    ad = pm[:, base + 2 * RW_DECAY_LORA:base + 2 * RW_DECAY_LORA + RW_ICLR_LORA]
    gd = pm[:, base + 2 * RW_DECAY_LORA + RW_ICLR_LORA:RW_COLS]
    a0, k_k, k_a, r_k = vec_ref[0:1, :], vec_ref[1:2, :], vec_ref[2:3, :], vec_ref[3:4, :]
    a = _sigmoid(a0 + _mm(ad, a2_ref[...]))
    g = _mm(_sigmoid(gd), g2_ref[...])
    kx = k * k_k
    kk = kx * lax.rsqrt(_head_sum(kx * kx, RW_HEAD) + 1e-6)
    k2 = k * (1.0 + (a - 1.0) * k_a)
    bonus = _head_sum(r * k2 * r_k, RW_HEAD) * v
    cols = [r, k2, v, -kk, kk * a]
    for d in range(2):
        wd_d = pm[:, base + d * RW_DECAY_LORA:base + (d + 1) * RW_DECAY_LORA]
        pre = vec_ref[4 + d:5 + d, :] + _mm(jnp.tanh(wd_d), w2_ref[d])
        cols.append(-jnp.exp(-_softplus(-pre) - 0.5))
    cols += [bonus, g]
    for j, cval in enumerate(cols):
        o_ref[:, j * RW_DIM:(j + 1) * RW_DIM] = cval


def _rw_prep_kernel(tiles_per_batch, n_lat_tiles, x_ref, u_ref, d_ref, mu_ref, vec_ref, a2_ref, g2_ref, w2_ref, o_ref):
    i = pl.program_id(0)
    x = x_ref[...]
    rows = x.shape[0]
    ridx = lax.broadcasted_iota(jnp.int32, x.shape, 0)
    lane = lax.broadcasted_iota(jnp.int32, x.shape, 1)
    before = _shift_rows(x, 1)
    after = _shift_rows(x, -1)
    col = ridx % GRID_W
    left = jnp.where(col == 0, 0.0, before)
    right = jnp.where(col == GRID_W - 1, 0.0, after)
    top = jnp.where(i % tiles_per_batch == 0, 0.0, u_ref[...])
    bot = jnp.where(i % tiles_per_batch == tiles_per_batch - 1, 0.0, d_ref[...])
    up = jnp.concatenate([top, x[:rows - GRID_W, :]], axis=0)
    down = jnp.concatenate([x[GRID_W:, :], bot], axis=0)
    sel = lane % 4
    shifted_lat = jnp.where(sel == 0, left, jnp.where(sel == 1, right, jnp.where(sel == 2, up, down)))
    prev = jnp.where(ridx == 0, 0.0, before)
    nxt = jnp.where(ridx == rows - 1, 0.0, after)
    shifted_ctx = jnp.where(lane % 2 == 0, prev, nxt)
    shifted = jnp.where(i < n_lat_tiles, shifted_lat, shifted_ctx)
    _rw_post(x, shifted, mu_ref, vec_ref, a2_ref, g2_ref, w2_ref, o_ref)


def _rw_prep(p_rw, mu, vec, a2, g2, w2, n_batch, seq, ctx_len):
    ntok, wd = p_rw.shape
    n_lat = n_batch * seq
    wo = RW_NOUT * RW_DIM
    tm = RW_TM
    assert ctx_len == tm and seq % tm == 0 and tm % GRID_W == 0
    per = tm // GRID_W
    last_blk = ntok // GRID_W - 1
    consts = [pl.BlockSpec(a.shape, (lambda i, nd=a.ndim: (0,) * nd)) for a in (mu, vec, a2, g2, w2)]
    return pl.pallas_call(
        functools.partial(_rw_prep_kernel, seq // tm, n_lat // tm),
        grid=(ntok // tm,),
        in_specs=[pl.BlockSpec((tm, wd), lambda i: (i, 0)),
                  pl.BlockSpec((GRID_W, wd), lambda i: (jnp.maximum(i * per - 1, 0), 0)),
                  pl.BlockSpec((GRID_W, wd), lambda i: (jnp.minimum((i + 1) * per, last_blk), 0))] + consts,
        out_specs=pl.BlockSpec((tm, wo), lambda i: (i, 0)),
        out_shape=jax.ShapeDtypeStruct((ntok, wo), F32),
        compiler_params=_cparams(1),
        name="rw_prep",
    )(p_rw, p_rw, p_rw, mu, vec, a2, g2, w2)


def _rw_scan_kernel(rf, kf, vf, af, bf, wf, rb, kb, vb, ab, bb, wb, yf_ref, yb_ref, st_ref):
    c = RW_CHUNK

    @pl.when(pl.program_id(1) == 0)
    def _():
        st_ref[...] = jnp.zeros_like(st_ref)

    ri, ci = _tri_iota(c)
    nr = RW_HEADS * c
    head_mask = (lax.broadcasted_iota(jnp.int32, (nr, RW_DIM), 0) // c
                 == lax.broadcasted_iota(jnp.int32, (nr, RW_DIM), 1) // RW_HEAD)

    def stacked(x):
        xb = x.astype(BF16)
        return jnp.where(head_mask, jnp.concatenate([xb] * RW_HEADS, axis=0), jnp.zeros((), BF16))

    dirs = ((rf, kf, vf, af, bf, wf), (rb, kb, vb, ab, bb, wb))
    chains = [(d, s) for d in range(2) for s in range(SCAN_SUB)]
    masks = [_block_tri(RW_HEADS, c, d == 1) for d in range(2)]
    ops = []
    for d, s in chains:
        incl_c = (ri >= ci) if d == 0 else (ri <= ci)
        last = c - 1 if d == 0 else 0
        r, k, v, a, b, lw = [ref[s * c:(s + 1) * c, :] for ref in dirs[d]]
        cum = _mm_exact_lhs(incl_c.astype(F32), lw)
        lc = cum[last:last + 1, :]
        e_neg = jnp.exp(-cum)
        e_rel = jnp.exp(lc - cum)
        ops.append(dict(rt=stacked(r * jnp.exp(cum)), at=stacked(a * jnp.exp(cum - lw)),
                        bt=stacked(b * e_neg), kt=stacked(k * e_neg), bl=stacked(b * e_rel), kl=stacked(k * e_rel),
                        vs=stacked(v), e_last=jnp.exp(lc)))
    raw = [(_mm_nt(o["at"], o["bt"]), _mm_nt(o["at"], o["kt"]), _mm_nt(o["rt"], o["bt"]), _mm_nt(o["rt"], o["kt"]))
           for o in ops]
    aab = [jnp.where(masks[d][1], x[0], 0.0) for (d, _), x in zip(chains, raw)]
    aak = [jnp.where(masks[d][1], x[1], 0.0) for (d, _), x in zip(chains, raw)]
    arb = [jnp.where(masks[d][0], x[2], 0.0) for (d, _), x in zip(chains, raw)]
    ark = [jnp.where(masks[d][0], x[3], 0.0) for (d, _), x in zip(chains, raw)]
    aak_v = [_mm(m, o["vs"]) for m, o in zip(aak, ops)]
    ark_v = [_mm(m, o["vs"]) for m, o in zip(ark, ops)]
    tinv = _neumann_many(aab, c)
    state = [st_ref[0], st_ref[1]]
    for step in range(SCAN_SUB):
        idx = [step, SCAN_SUB + (SCAN_SUB - 1 - step)]
        a_s = [_mm_nt(ops[i]["at"], state[d]) for d, i in enumerate(idx)]
        r_s = [_mm_nt(ops[i]["rt"], state[d]) for d, i in enumerate(idx)]
        u = [_mm(tinv[i], a_s[d] + aak_v[i]) for d, i in enumerate(idx)]
        y = [r_s[d] + _mm(arb[i], u[d]) + ark_v[i] for d, i in enumerate(idx)]
        state = [state[d] * ops[i]["e_last"] + _mm_tn(u[d], ops[i]["bl"]) + _mm_tn(ops[i]["vs"], ops[i]["kl"])
                 for d, i in enumerate(idx)]
        for d, (i, y_ref) in enumerate(zip(idx, (yf_ref, yb_ref))):
            s = chains[i][1]
            y_ref[s * c:(s + 1) * c, :] = functools.reduce(
                lambda p, q: p + q, [y[d][h * c:(h + 1) * c] for h in range(RW_HEADS)])
    st_ref[0] = state[0]
    st_ref[1] = state[1]


def _rw_scan(rwp, n_batch, seq, ctx_len):
    ntok = rwp.shape[0]
    c = RW_CHUNK * SCAN_SUB
    fwd, bwd, steps = _chunk_maps(n_batch * seq, seq, ctx_len, c)

    def specs(m, wcol):
        return [pl.BlockSpec((c, RW_DIM), lambda b, t, j=j: (m(b, t), j)) for j in (0, 1, 2, 3, 4, wcol)]

    return pl.pallas_call(
        _rw_scan_kernel,
        grid=(n_batch, steps),
        in_specs=specs(fwd, 5) + specs(bwd, 6),
        out_specs=[pl.BlockSpec((c, RW_DIM), lambda b, t: (fwd(b, t), 0)),
                   pl.BlockSpec((c, RW_DIM), lambda b, t: (bwd(b, t), 0))],
        out_shape=[jax.ShapeDtypeStruct((ntok, RW_DIM), F32)] * 2,
        scratch_shapes=[pltpu.VMEM((2, RW_DIM, RW_DIM), F32)],
        compiler_params=_cparams(2),
        name="rw_scan",
    )(*([rwp] * 12))


def _outproj_kernel(n_lat_tiles, x_ref, ymf, ymb, xs_ref, z_ref, odf, odb, ocf, ocb, gate_ref, yrf, yrb, bonus_ref,
                    g_ref, vec_ref, dnw_ref, wout_ref, gmsa_ref, n2w_ref, sh_ref, sc_ref, wr_ref, br_ref,
                    xo_ref, h2_ref, route_ref, cnt_ref, run_ref):
    tm = x_ref.shape[0]

    @pl.when(pl.program_id(0) == 0)
    def _():
        run_ref[...] = jnp.zeros_like(run_ref)

    ym = (xs_ref[...] * vec_ref[0:1, :] + ymf[...] + ymb[...]) * _silu(z_ref[...])
    gw = M_INNER // M_GROUPS
    parts = []
    for g in range(M_GROUPS):
        yg = ym[:, g * gw:(g + 1) * gw]
        parts.append(yg * lax.rsqrt(jnp.mean(yg * yg, axis=-1, keepdims=True) + 1e-5))
    mix_m = jnp.concatenate(parts, axis=1) * vec_ref[1:2, :]
    od = jnp.where(pl.program_id(0) < n_lat_tiles, odf[...] + odb[...], ocf[...] + ocb[...])
    parts = []
    for h in range(DN_HEADS):
        oh = od[:, h * DN_HEAD_V:(h + 1) * DN_HEAD_V]
        parts.append(oh * lax.rsqrt(jnp.mean(oh * oh, axis=-1, keepdims=True) + 1e-6))
    mix_d = jnp.concatenate(parts, axis=1) * dnw_ref[...] * _silu(gate_ref[...])
    yr = yrf[...] + yrb[...]
    mean = _head_sum(yr, RW_HEAD) * (1.0 / RW_HEAD)
    cen = yr - mean
    var = _head_sum(cen * cen, RW_HEAD) * (1.0 / RW_HEAD)
    mix_r = ((cen * lax.rsqrt(var + RW_LN_EPS)) * vec_ref[2:3, :] + vec_ref[3:4, :] + bonus_ref[...]) * g_ref[...]

    attn = (_mm(mix_m, wout_ref[0:M_INNER, :]) + _mm(mix_d, wout_ref[M_INNER:M_INNER + DN_VD, :])
            + _mm(mix_r, wout_ref[M_INNER + DN_VD:, :]))
    xn = x_ref[...] + gmsa_ref[...] * attn
    xo_ref[...] = xn
    h2 = xn * lax.rsqrt(jnp.mean(xn * xn, axis=-1, keepdims=True) + NORM_EPS) * n2w_ref[...]
    h2 = h2 * (1.0 + sc_ref[...]) + sh_ref[...]
    h2_ref[...] = h2

    logits = _mm3(h2, wr_ref[...]) + br_ref[...]
    lane = lax.broadcasted_iota(jnp.int32, (tm, LANES), 1)
    neg = jnp.float32(-jnp.inf)
    lg = jnp.where(lane < N_EXPERTS, logits, neg)
    tops, idxs = [], []
    onehot = jnp.zeros((tm, LANES), F32)
    for _ in range(TOP_K):
        m = jnp.max(lg, axis=-1, keepdims=True)
        idx = jnp.min(jnp.where(lg == m, lane, LANES), axis=-1, keepdims=True)
        sel = lane == idx
        tops.append(m)
        idxs.append(idx)
        lg = jnp.where(sel, neg, lg)
        onehot = onehot + sel.astype(F32)
    ri, ci = _tri_iota(tm)
    before = jnp.dot((ri > ci).astype(BF16), onehot.astype(BF16), preferred_element_type=F32) + run_ref[0:1, :]
    run_ref[0:1, :] = run_ref[0:1, :] + jnp.sum(onehot, axis=0, keepdims=True)
    exps = [jnp.exp(m - tops[0]) for m in tops]
    denom = exps[0] + exps[1] + exps[2] + exps[3]
    route = jnp.zeros((tm, LANES), F32)
    for k in range(TOP_K):
        rank = jnp.sum(jnp.where(lane == idxs[k], before, 0.0), axis=-1, keepdims=True)
        route = jnp.where(lane == k, idxs[k].astype(F32), route)
        route = jnp.where(lane == TOP_K + k, exps[k] / denom, route)
        route = jnp.where(lane == 2 * TOP_K + k, rank, route)
    route_ref[...] = route
    cnt_ref[...] = run_ref[...]


def _outproj(x_all, ssd_y, xbc, p_m, dn_o, p_gate, rw_y, rwp, vec, dnw, w_out, mod3, layer, norm2_w,
             w_router, b_router, n_rows, seq, n_batch):
    d = x_all.shape[1]
    tm = OUT_TM
    row = lambda wd, col=0: pl.BlockSpec((tm, wd), lambda i, col=col: (i, col))
    const = lambda a: pl.BlockSpec(a.shape, lambda i: (0,) * a.ndim)
    n2 = norm2_w.reshape(1, d)
    n_lat_tiles = n_batch * seq // tm
    lat_row = pl.BlockSpec((tm, DN_VD), lambda i: (jnp.minimum(i, n_lat_tiles - 1), 0))
    ctx_row = pl.BlockSpec((tm, DN_VD), lambda i: (jnp.maximum(i - n_lat_tiles, 0), 0))
    ins = [x_all, ssd_y[0], ssd_y[1], xbc, p_m, dn_o[0], dn_o[1], dn_o[2], dn_o[3], p_gate, rw_y[0], rw_y[1],
           rwp, rwp, vec, dnw, w_out, mod3, n2, mod3, mod3, w_router, b_router]
    specs = [row(d), row(M_INNER), row(M_INNER), row(M_INNER, 0), row(M_INNER, M_CONV_DIM // M_INNER),
             lat_row, lat_row, ctx_row, ctx_row, row(DN_VD), row(RW_DIM), row(RW_DIM), row(RW_DIM, 7),
             row(RW_DIM, 8), const(vec), const(dnw), const(w_out), _mod_spec(layer, 2, tm, seq, n_batch),
             const(n2), _mod_spec(layer, 3, tm, seq, n_batch), _mod_spec(layer, 4, tm, seq, n_batch),
             const(w_router), const(b_router)]
    return pl.pallas_call(
        functools.partial(_outproj_kernel, n_lat_tiles),
        grid=(n_rows // tm,),
        in_specs=specs,
        out_specs=[row(d), row(d), row(LANES), pl.BlockSpec((8, LANES), lambda i: (0, 0))],
        out_shape=[jax.ShapeDtypeStruct((n_rows, d), F32), jax.ShapeDtypeStruct((n_rows, d), F32),
                   jax.ShapeDtypeStruct((n_rows, LANES), F32), jax.ShapeDtypeStruct((8, LANES), F32)],
        scratch_shapes=[pltpu.VMEM((8, LANES), F32)],
        compiler_params=_cparams(1),
        name="finish_outproj_router",
    )(*ins)


def _dispatch_kernel(dest_ref, h_ref, zeros_ref, xs_ref, sem):
    del zeros_ref
    tm = h_ref.shape[0]

    def copy(i, k):
        return pltpu.make_async_copy(h_ref.at[pl.ds(i, 1)], xs_ref.at[pl.ds(dest_ref[0, i * TOP_K + k], 1)], sem)

    def start(i, carry):
        for k in range(TOP_K):
            copy(i, k).start(priority=k % 2)
        return carry

    def wait(i, carry):
        for k in range(TOP_K):
            copy(i, k).wait()
        return carry

    lax.fori_loop(0, tm, start, 0)
    lax.fori_loop(0, tm, wait, 0)


def _dispatch(h2, dest, n_slots):
    n, d = h2.shape
    tm = MOE_TM
    return pl.pallas_call(
        _dispatch_kernel,
        grid=(n // tm,),
        in_specs=[pl.BlockSpec((None, 1, tm * TOP_K), lambda i: (i, 0, 0), memory_space=pltpu.SMEM),
                  pl.BlockSpec((tm, d), lambda i: (i, 0)),
                  pl.BlockSpec(memory_space=pl.ANY)],
        out_specs=pl.BlockSpec(memory_space=pl.ANY),
        out_shape=jax.ShapeDtypeStruct((n_slots, d), F32),
        scratch_shapes=[pltpu.SemaphoreType.DMA(())],
        input_output_aliases={2: 0},
        compiler_params=_cparams(1),
        name="moe_dispatch",
    )(dest.reshape(n // tm, 1, tm * TOP_K), h2, jnp.zeros((n_slots, d), F32))


def _moe_kernel(be_ref, nv_ref, x_ref, wgu_ref, bgu_ref, wdn_ref, bdn_ref, y_ref, wgu_bf, wdn_bf):
    j = pl.program_id(0)

    @pl.when(j < nv_ref[0])
    def _():
        changed = jnp.logical_or(j == 0, be_ref[j] != be_ref[jnp.maximum(j - 1, 0)])

        @pl.when(changed)
        def _():
            wgu_bf[...] = wgu_ref[...].astype(BF16)
            wdn_bf[...] = wdn_ref[...].astype(BF16)

        gu = jnp.dot(x_ref[...].astype(BF16), wgu_bf[...], preferred_element_type=F32) + bgu_ref[...]
        g_ = jnp.minimum(gu[:, :D_FF], SWIGLU_LIMIT)
        u_ = jnp.clip(gu[:, D_FF:], -SWIGLU_LIMIT, SWIGLU_LIMIT)
        act = g_ * _sigmoid(SWIGLU_ALPHA * g_) * (u_ + 1.0)
        y_ref[...] = jnp.dot(act.astype(BF16), wdn_bf[...], preferred_element_type=F32) + bdn_ref[...]

    @pl.when(j >= nv_ref[0])
    def _():
        y_ref[...] = jnp.zeros_like(y_ref)


def _moe(xs, block_expert, n_valid, layer, w_gate_up, b_gate_up, w_down, b_down):
    n_slots, d = xs.shape
    bm = MOE_BLOCK
    n_layers, n_exp, _, f2 = w_gate_up.shape
    blk = lambda j, be, nv: (jnp.minimum(j, nv[0] - 1), 0)
    grid_spec = pltpu.PrefetchScalarGridSpec(
        num_scalar_prefetch=2,
        grid=(n_slots // bm,),
        in_specs=[pl.BlockSpec((bm, d), blk),
                  pl.BlockSpec((None, None, d, f2), lambda j, be, nv: (layer, be[j], 0, 0)),
                  pl.BlockSpec((None, None, 1, f2), lambda j, be, nv: (layer, be[j], 0, 0)),
                  pl.BlockSpec((None, None, f2 // 2, d), lambda j, be, nv: (layer, be[j], 0, 0)),
                  pl.BlockSpec((None, None, 1, d), lambda j, be, nv: (layer, be[j], 0, 0))],
        out_specs=pl.BlockSpec((bm, d), lambda j, be, nv: (j, 0)),
        scratch_shapes=[pltpu.VMEM((d, f2), BF16), pltpu.VMEM((f2 // 2, d), BF16)],
    )
    return pl.pallas_call(
        _moe_kernel,
        grid_spec=grid_spec,
        out_shape=jax.ShapeDtypeStruct((n_slots, d), F32),
        compiler_params=_cparams(1),
        name="moe_experts",
    )(block_expert, n_valid, xs, w_gate_up, b_gate_up.reshape(n_layers, n_exp, 1, f2), w_down,
      b_down.reshape(n_layers, n_exp, 1, d))


def _combine_kernel(final_norm, dest_ref, x_ref, route_ref, g_ref, nf_ref, y_ref, o_ref, buf, sem):
    tm = x_ref.shape[0]

    def copy(i, k):
        return pltpu.make_async_copy(y_ref.at[pl.ds(dest_ref[0, i * TOP_K + k], 1)], buf.at[k, pl.ds(i, 1)], sem)

    def start(i, carry):
        for k in range(TOP_K):
            copy(i, k).start(priority=k % 2)
        return carry

    def wait(i, carry):
        for k in range(TOP_K):
            copy(i, k).wait()
        return carry

    lax.fori_loop(0, tm, start, 0)
    lax.fori_loop(0, tm, wait, 0)
    route = route_ref[...]
    acc = jnp.zeros(x_ref.shape, F32)
    for k in range(TOP_K):
        acc = acc + route[:, TOP_K + k:TOP_K + k + 1] * buf[k]
    out = x_ref[...] + g_ref[...] * acc
    if final_norm:
        out = out * lax.rsqrt(jnp.mean(out * out, axis=-1, keepdims=True) + NORM_EPS) * nf_ref[...]
    o_ref[...] = out


def _combine(x_new, route, dest, y_slots, mod3, layer, norm_f_w, final_norm, seq, n_batch):
    n, d = x_new.shape
    tm = MOE_TM
    row = lambda wd: pl.BlockSpec((tm, wd), lambda i: (i, 0))
    return pl.pallas_call(
        functools.partial(_combine_kernel, final_norm),
        grid=(n // tm,),
        in_specs=[pl.BlockSpec((None, 1, tm * TOP_K), lambda i: (i, 0, 0), memory_space=pltpu.SMEM),
                  row(d), row(LANES), _mod_spec(layer, 5, tm, seq, n_batch),
                  pl.BlockSpec((1, d), lambda i: (0, 0)),
                  pl.BlockSpec(memory_space=pl.ANY)],
        out_specs=row(d),
        out_shape=jax.ShapeDtypeStruct((n, d), F32),
        scratch_shapes=[pltpu.VMEM((TOP_K, tm, d), F32), pltpu.SemaphoreType.DMA(())],
        compiler_params=_cparams(1),
        name="moe_combine",
    )(dest.reshape(n // tm, 1, tm * TOP_K), x_new, route, mod3, norm_f_w.reshape(1, d), y_slots)


def _routing_tables(route, counts, n_slots):
    bm = MOE_BLOCK
    top_e = route[:, :TOP_K].astype(jnp.int32)
    rank = route[:, 2 * TOP_K:3 * TOP_K].astype(jnp.int32)
    cnt = counts[0, :N_EXPERTS].astype(jnp.int32)
    padded = (cnt + bm - 1) // bm * bm
    ends = jnp.cumsum(padded)
    dest = (ends - padded)[top_e] + rank
    n_blocks = n_slots // bm
    n_valid = ends[-1] // bm
    blk = jnp.minimum(jnp.arange(n_blocks, dtype=jnp.int32), n_valid - 1)
    block_expert = jnp.sum((blk[:, None] >= (ends // bm)[None, :]).astype(jnp.int32), axis=1)
    block_expert = jnp.minimum(block_expert, N_EXPERTS - 1)
    return dest.reshape(-1), block_expert, n_valid.reshape(1).astype(jnp.int32)


def _pad_cols(a, width):
    return jnp.pad(a, [(0, 0)] * (a.ndim - 1) + [(0, width - a.shape[-1])])


def kernel(x, c, ctx, c_ctx, w_mod, b_mod, norm1_w, norm2_w, w_in, w_out, m_conv_w, m_conv_b, m_dt_bias, m_a_log, m_d, m_norm_w, dn_conv_w, dn_dt_bias, dn_a_log, dn_norm_w, rw_mu, rw_w0, rw_w2, rw_a0, rw_a2, rw_g2, rw_k_k, rw_k_a, rw_r_k, rw_ln_w, rw_ln_b, w_router, b_router, w_gate_up, b_gate_up, w_down, b_down, norm_f_w):
    n_batch, seq, d = x.shape
    ctx_len = ctx.shape[1]
    depth = w_mod.shape[0]
    n_lat, n_ctx = n_batch * seq, n_batch * ctx_len
    assert d == D_MODEL and seq // GRID_W == DN_CHUNK and n_batch + 1 <= 8

    x_all = jnp.concatenate([x.reshape(n_lat, d), ctx.reshape(n_ctx, d)], axis=0)
    c8 = jnp.zeros((8, d), F32).at[:n_batch].set(c).at[n_batch].set(c_ctx)
    mod3 = _modulation(c8, w_mod, b_mod).reshape(depth * 6 * 8, 1, d)

    out = None
    for i in range(depth):
        last = i == depth - 1
        wi = w_in[i]
        mo, do, ro = 0, M_COLS, M_COLS + DN_COLS
        w_m = jnp.concatenate([wi[:, mo + M_INNER:mo + M_INNER + M_CONV_DIM], wi[:, mo:mo + M_INNER]], axis=1)
        w_q = wi[:, do:do + DN_KD]
        w_k = wi[:, do + DN_KD:do + 2 * DN_KD]
        w_v = wi[:, do + 2 * DN_KD:do + DN_CONV_DIM]
        w_g = wi[:, do + DN_CONV_DIM:do + DN_CONV_DIM + DN_VD]
        w_r = _pad_cols(wi[:, ro:ro + RW_COLS], RW_COLS_PAD)
        w_s = _pad_cols(jnp.concatenate([wi[:, mo + M_INNER + M_CONV_DIM:mo + M_COLS],
                                         wi[:, do + DN_CONV_DIM + DN_VD:do + DN_COLS]], axis=1), LANES)
        weights = [w.astype(BF16) for w in (w_m, w_q, w_k, w_v, w_g, w_r, w_s)]
        p_m, p_q, p_k, p_v, p_gate, p_rw, small = _inproj(x_all, norm1_w[i], mod3, i, weights, seq, n_batch)

        xbc = _ssd_prep(p_m, m_conv_w[i], m_conv_b[i], seq, n_lat, ctx_len)
        ssd_par = jnp.zeros((8, LANES), F32)
        ssd_par = ssd_par.at[0, :2 * M_HEADS].set(m_dt_bias[i].reshape(-1))
        ssd_par = ssd_par.at[1, :2 * M_HEADS].set(-jnp.exp(m_a_log[i].astype(F32)).reshape(-1))
        ssd_y = _ssd_scan(xbc, small, ssd_par, n_batch, seq, ctx_len)

        lat_qkv, ctx_qkv = _dn_prep(p_q, p_k, p_v, dn_conv_w[i], n_batch, seq, ctx_len)
        dn_par = jnp.zeros((8, LANES), F32)
        dn_par = dn_par.at[0, 16:16 + 2 * DN_HEADS].set(dn_dt_bias[i].reshape(-1))
        dn_par = dn_par.at[1, 16:16 + 2 * DN_HEADS].set(-jnp.exp(dn_a_log[i].astype(F32)).reshape(-1))
        dn_o = _dn_scan(lat_qkv, ctx_qkv, small, dn_par, n_batch, seq, ctx_len)

        rw_vec = jnp.stack([rw_a0[i], rw_k_k[i], rw_k_a[i], rw_r_k[i].reshape(-1), rw_w0[i, 0], rw_w0[i, 1],
                            jnp.zeros((RW_DIM,), F32), jnp.zeros((RW_DIM,), F32)])
        rwp = _rw_prep(p_rw, _pad_cols(rw_mu[i], RW_COLS_PAD).reshape(1, -1), rw_vec, rw_a2[i], rw_g2[i], rw_w2[i],
                       n_batch, seq, ctx_len)
        rw_y = _rw_scan(rwp, n_batch, seq, ctx_len)

        n_rows = n_lat if last else n_lat + n_ctx
        fin_vec = jnp.stack([jnp.repeat(m_d[i], M_HEAD_DIM), m_norm_w[i], rw_ln_w[i], rw_ln_b[i]]
                            + [jnp.zeros((RW_DIM,), F32)] * 4)
        dnw = jnp.tile(dn_norm_w[i], DN_HEADS).reshape(1, DN_VD)
        x_new, h2, route, counts = _outproj(
            x_all, ssd_y, xbc, p_m, dn_o, p_gate, rw_y, rwp, fin_vec, dnw, w_out[i].astype(BF16), mod3, i,
            norm2_w[i], _pad_cols(w_router[i], LANES), _pad_cols(b_router[i].reshape(1, -1), LANES),
            n_rows, seq, n_batch)

        n_slots = (n_rows * TOP_K // MOE_BLOCK + N_EXPERTS) * MOE_BLOCK
        dest, block_expert, n_valid = _routing_tables(route, counts, n_slots)
        xs = _dispatch(h2, dest, n_slots)
        y_slots = _moe(xs, block_expert, n_valid, i, w_gate_up, b_gate_up, w_down, b_down)
        res = _combine(x_new, route, dest, y_slots, mod3, i, norm_f_w, last, seq, n_batch)
        if last:
            out = res
        else:
            x_all = res
    return out.reshape(n_batch, seq, d)
```

```python
import functools
import math

import jax
import jax.numpy as jnp
from jax import lax
from jax.experimental import pallas as pl
from jax.experimental.pallas import tpu as pltpu

F32 = jnp.float32
BF16 = jnp.bfloat16

D_MODEL = 1024
GRID_W = 64
NORM_EPS = 1e-6

M_HEADS, M_HEAD_DIM, M_STATE, M_GROUPS = 4, 64, 128, 2
M_INNER = M_HEADS * M_HEAD_DIM
M_CONV_DIM = M_INNER + 2 * M_GROUPS * M_STATE
M_COLS = M_INNER + M_CONV_DIM + 2 * M_HEADS
M_CHUNK = 128

DN_HEADS, DN_HEAD_K, DN_HEAD_V = 4, 128, 128
DN_KD = DN_HEADS * DN_HEAD_K
DN_VD = DN_HEADS * DN_HEAD_V
DN_CONV_DIM = 2 * DN_KD + DN_VD
DN_COLS = DN_CONV_DIM + DN_VD + 4 * DN_HEADS
DN_CHUNK = 64

RW_HEADS, RW_HEAD = 4, 64
RW_DIM = RW_HEADS * RW_HEAD
RW_DECAY_LORA, RW_ICLR_LORA, RW_GATE_LORA = 64, 64, 128
RW_COLS = 3 * RW_DIM + 2 * RW_DECAY_LORA + RW_ICLR_LORA + RW_GATE_LORA
RW_COLS_PAD = 1152
RW_LN_EPS = RW_HEAD * 1e-5
RW_CHUNK = 64
RW_NOUT = 9

N_EXPERTS, TOP_K = 32, 4
D_FF = 1024
SWIGLU_ALPHA, SWIGLU_LIMIT = 1.702, 7.0
MOE_BLOCK = 512

LANES = 128
VMEM_LIMIT = 56 * 1024 * 1024

IN_TM = 512
CONV_TM = 256
DN_G = 8
RW_TM = 256
OUT_TM = 256
MOE_TM = 256
SCAN_SUB = 4


def _cparams(n_axes):
    return pltpu.CompilerParams(dimension_semantics=("arbitrary",) * n_axes, vmem_limit_bytes=VMEM_LIMIT)


def _sigmoid(x):
    return 1.0 / (1.0 + jnp.exp(-x))


def _silu(x):
    return x * _sigmoid(x)


def _softplus(x):
    return jnp.maximum(x, 0.0) + jnp.log(1.0 + jnp.exp(-jnp.abs(x)))


def _mm(a, b):
    return jnp.dot(a.astype(BF16), b.astype(BF16), preferred_element_type=F32)


def _mm_nt(a, b):
    return lax.dot_general(a.astype(BF16), b.astype(BF16), (((1,), (1,)), ((), ())), preferred_element_type=F32)


def _mm_tn(a, b):
    return lax.dot_general(a.astype(BF16), b.astype(BF16), (((0,), (0,)), ((), ())), preferred_element_type=F32)


def _split3(x):
    hi = x.astype(BF16)
    r1 = x - hi.astype(F32)
    mid = r1.astype(BF16)
    lo = (r1 - mid.astype(F32)).astype(BF16)
    return hi, mid, lo


def _mm3(a, b):
    ah, am, _ = _split3(a)
    bh, bm, _ = _split3(b)
    dot = functools.partial(jnp.dot, preferred_element_type=F32)
    return dot(ah, bh) + (dot(am, bh) + dot(ah, bm))


def _mm_exact_lhs(a01, x):
    a = a01.astype(BF16)
    dot = functools.partial(jnp.dot, preferred_element_type=F32)
    hi, mid, lo = _split3(x)
    return dot(a, hi) + (dot(a, mid) + dot(a, lo))


def _mm_exact_rhs(x, b01):
    b = b01.astype(BF16)
    dot = functools.partial(jnp.dot, preferred_element_type=F32)
    hi, mid, lo = _split3(x)
    return dot(hi, b) + (dot(mid, b) + dot(lo, b))


def _to_rows(x, n_rows):
    sel = (lax.broadcasted_iota(jnp.int32, (n_rows, LANES), 0)
           == lax.broadcasted_iota(jnp.int32, (n_rows, LANES), 1)).astype(BF16)
    nt = functools.partial(lax.dot_general, dimension_numbers=(((1,), (1,)), ((), ())), preferred_element_type=F32)
    hi, mid, lo = _split3(x)
    return nt(sel, hi) + (nt(sel, mid) + nt(sel, lo))


def _tri_iota(c):
    return lax.broadcasted_iota(jnp.int32, (c, c), 0), lax.broadcasted_iota(jnp.int32, (c, c), 1)


def _neumann_many(ms, nil):
    ri, ci = _tri_iota(ms[0].shape[0])
    eye = (ri == ci).astype(F32)
    sums = [eye + m for m in ms]
    pows = list(ms)
    for _ in range(int(math.log2(nil)) - 1):
        pows = [_mm(p, p) for p in pows]
        sums = [s + _mm(s, p) for s, p in zip(sums, pows)]
    return sums


def _stack_rows(x, n, width):
    return jnp.concatenate([x[:, h * width:(h + 1) * width] for h in range(n)], axis=0)


def _block_tri(n_blocks, c, reverse):
    ri, ci = _tri_iota(n_blocks * c)
    same = (ri // c) == (ci // c)
    if reverse:
        return jnp.logical_and(same, ri <= ci), jnp.logical_and(same, ri < ci)
    return jnp.logical_and(same, ri >= ci), jnp.logical_and(same, ri > ci)


def _shift_rows(x, n):
    return pltpu.roll(x, n % x.shape[0], 0)


def _conv3(x, prev_row, next_row, w):
    rows = x.shape[0]
    ridx = lax.broadcasted_iota(jnp.int32, x.shape, 0)
    up = jnp.where(ridx == 0, prev_row, _shift_rows(x, 1))
    dn = jnp.where(ridx == rows - 1, next_row, _shift_rows(x, -1))
    return up * w[0:1, :] + x * w[1:2, :] + dn * w[2:3, :]


def _head_sum(x, width):
    n = x.shape[1]
    ri, ci = _tri_iota(n)
    return _mm_exact_rhs(x, (ri // width == ci // width).astype(F32))


def _mod_kernel(c_ref, w_ref, b_ref, o_ref):
    o_ref[...] = _mm3(_silu(c_ref[...]), w_ref[...]) + b_ref[...]


def _modulation(c8, w_mod, b_mod):
    n_layers, d = w_mod.shape[0], w_mod.shape[1]
    return pl.pallas_call(
        _mod_kernel,
        grid=(n_layers, 6),
        in_specs=[
            pl.BlockSpec((8, d), lambda l, j: (0, 0)),
            pl.BlockSpec((None, d, d), lambda l, j: (l, 0, j)),
            pl.BlockSpec((None, None, 1, d), lambda l, j: (l, j, 0, 0)),
        ],
        out_specs=pl.BlockSpec((None, None, 8, d), lambda l, j: (l, j, 0, 0)),
        out_shape=jax.ShapeDtypeStruct((n_layers, 6, 8, d), F32),
        compiler_params=_cparams(2),
        name="adaln_mod",
    )(c8, w_mod, b_mod.reshape(n_layers, 6, 1, d))


def _mod_spec(layer, which, rows_per_tile, seq, n_batch):
    base = (layer * 6 + which) * 8
    tiles_per_batch = seq // rows_per_tile

    def imap(i, *_):
        return (base + jnp.minimum(i // tiles_per_batch, n_batch), 0, 0)

    return pl.BlockSpec((None, 1, D_MODEL), imap)


def _inproj_kernel(x_ref, nw_ref, sh_ref, sc_ref, *refs):
    n = len(refs) // 2
    x = x_ref[...]
    h = x * lax.rsqrt(jnp.mean(x * x, axis=-1, keepdims=True) + NORM_EPS) * nw_ref[...]
    h = (h * (1.0 + sc_ref[...]) + sh_ref[...]).astype(BF16)
    for w_ref, o_ref in zip(refs[:n], refs[n:]):
        o_ref[...] = jnp.dot(h, w_ref[...], preferred_element_type=F32)


def _inproj(x_all, norm_w, mod3, layer, weights, seq, n_batch):
    ntok, d = x_all.shape
    widths = [w.shape[1] for w in weights]
    row = lambda wd: pl.BlockSpec((IN_TM, wd), lambda i: (i, 0))
    return pl.pallas_call(
        _inproj_kernel,
        grid=(ntok // IN_TM,),
        in_specs=[row(d), pl.BlockSpec((1, d), lambda i: (0, 0)),
                  _mod_spec(layer, 0, IN_TM, seq, n_batch), _mod_spec(layer, 1, IN_TM, seq, n_batch)]
                 + [pl.BlockSpec(w.shape, lambda i: (0, 0)) for w in weights],
        out_specs=[row(wd) for wd in widths],
        out_shape=[jax.ShapeDtypeStruct((ntok, wd), F32) for wd in widths],
        compiler_params=_cparams(1),
        name="norm_mod_inproj",
    )(x_all, norm_w.reshape(1, d), mod3, mod3, *weights)


def _chunk_maps(n_lat_rows, seq, ctx_len, chunk):
    lat_chunks, ctx_chunks = seq // chunk, ctx_len // chunk
    ctx_base = n_lat_rows // chunk

    def fwd(b, t):
        return jnp.where(t < ctx_chunks, ctx_base + b * ctx_chunks + t, b * lat_chunks + (t - ctx_chunks))

    def bwd(b, t):
        return jnp.where(t < ctx_chunks, ctx_base + b * ctx_chunks + (ctx_chunks - 1 - t),
                         b * lat_chunks + (lat_chunks - 1 - (t - ctx_chunks)))

    return fwd, bwd, lat_chunks + ctx_chunks


def _ssd_prep_kernel(tiles_per_batch, n_lat_tiles, x_ref, p_ref, n_ref, w_ref, b_ref, o_ref):
    i = pl.program_id(0)
    lat = i < n_lat_tiles
    is_start = jnp.logical_or(jnp.logical_not(lat), i % tiles_per_batch == 0)
    is_end = jnp.logical_or(jnp.logical_not(lat), i % tiles_per_batch == tiles_per_batch - 1)
    prev_row = jnp.where(is_start, 0.0, p_ref[7:8, :])
    next_row = jnp.where(is_end, 0.0, n_ref[0:1, :])
    o_ref[...] = _silu(_conv3(x_ref[...], prev_row, next_row, w_ref[...]) + b_ref[...])


def _ssd_prep(p_m, conv_w, conv_b, seq, n_lat, ctx_len):
    ntok = p_m.shape[0]
    wd = M_CONV_DIM
    assert ctx_len == CONV_TM and seq % CONV_TM == 0
    sub = CONV_TM // 8
    last8 = ntok // 8 - 1
    kern = functools.partial(_ssd_prep_kernel, seq // CONV_TM, n_lat // CONV_TM)
    return pl.pallas_call(
        kern,
        grid=(ntok // CONV_TM,),
        in_specs=[
            pl.BlockSpec((CONV_TM, wd), lambda i: (i, 0)),
            pl.BlockSpec((8, wd), lambda i: (jnp.maximum(i * sub - 1, 0), 0)),
            pl.BlockSpec((8, wd), lambda i: (jnp.minimum((i + 1) * sub, last8), 0)),
            pl.BlockSpec((3, wd), lambda i: (0, 0)),
            pl.BlockSpec((1, wd), lambda i: (0, 0)),
        ],
        out_specs=pl.BlockSpec((CONV_TM, wd), lambda i: (i, 0)),
        out_shape=jax.ShapeDtypeStruct((ntok, wd), F32),
        compiler_params=_cparams(1),
        name="ssd_conv_silu",
    )(p_m, p_m, p_m, conv_w, conv_b.reshape(1, wd))


def _ssd_scan_kernel(xf, bf, cf, sf, xb, bb, cb, sb, par_ref, yf_ref, yb_ref, st_ref):
    q = M_CHUNK

    @pl.when(pl.program_id(1) == 0)
    def _():
        st_ref[...] = jnp.zeros_like(st_ref)

    ri, ci = _tri_iota(q)
    for d, (x_ref, b_ref, c_ref, s_ref, y_ref) in enumerate(((xf, bf, cf, sf, yf_ref), (xb, bb, cb, sb, yb_ref))):
        mask = (ri >= ci) if d == 0 else (ri <= ci)
        last = q - 1 if d == 0 else 0
        dt = _softplus(s_ref[...] + par_ref[0:1, :])
        la = dt * par_ref[1:2, :]
        cum = _mm_exact_lhs(mask.astype(F32), la)
        cum_rows = _to_rows(cum, 8)
        xs, bm, cm = x_ref[...], b_ref[...], c_ref[...]
        gmat = [_mm_nt(cm[:, g * M_STATE:(g + 1) * M_STATE], bm[:, g * M_STATE:(g + 1) * M_STATE])
                for g in range(M_GROUPS)]
        outs = []
        for h in range(M_HEADS):
            c = d * M_HEADS + h
            g = h // (M_HEADS // M_GROUPS)
            cc = cum[:, c:c + 1]
            cr = cum_rows[c:c + 1, :]
            seg = jnp.where(mask, jnp.exp(jnp.minimum(cc - cr, 0.0)), 0.0)
            xin = xs[:, h * M_HEAD_DIM:(h + 1) * M_HEAD_DIM] * dt[:, c:c + 1]
            bm_g = bm[:, g * M_STATE:(g + 1) * M_STATE]
            cm_g = cm[:, g * M_STATE:(g + 1) * M_STATE]
            state = st_ref[d, h]
            outs.append(_mm(gmat[g] * seg, xin) + _mm(cm_g, state) * jnp.exp(cc))
            lc = cum[last:last + 1, c:c + 1]
            st_ref[d, h] = jnp.exp(lc) * state + _mm_tn(bm_g * jnp.exp(lc - cc), xin)
        y_ref[...] = jnp.concatenate(outs, axis=1)


def _ssd_scan(xbc, small, par, n_batch, seq, ctx_len):
    ntok = xbc.shape[0]
    q = M_CHUNK
    fwd, bwd, steps = _chunk_maps(n_batch * seq, seq, ctx_len, q)

    def specs(m):
        return [pl.BlockSpec((q, M_INNER), lambda b, t: (m(b, t), 0)),
                pl.BlockSpec((q, M_INNER), lambda b, t: (m(b, t), 1)),
                pl.BlockSpec((q, M_INNER), lambda b, t: (m(b, t), 2)),
                pl.BlockSpec((q, LANES), lambda b, t: (m(b, t), 0))]

    return pl.pallas_call(
        _ssd_scan_kernel,
        grid=(n_batch, steps),
        in_specs=specs(fwd) + specs(bwd) + [pl.BlockSpec((8, LANES), lambda b, t: (0, 0))],
        out_specs=[pl.BlockSpec((q, M_INNER), lambda b, t: (fwd(b, t), 0)),
                   pl.BlockSpec((q, M_INNER), lambda b, t: (bwd(b, t), 0))],
        out_shape=[jax.ShapeDtypeStruct((ntok, M_INNER), F32)] * 2,
        scratch_shapes=[pltpu.VMEM((2, M_HEADS, M_STATE, M_HEAD_DIM), F32)],
        compiler_params=_cparams(2),
        name="ssd_scan",
    )(xbc, xbc, xbc, small, xbc, xbc, xbc, small, par)


def _dn_norm(x, which):
    if which == 2:
        return x
    outs = []
    for h in range(DN_HEADS):
        xh = x[:, h * DN_HEAD_K:(h + 1) * DN_HEAD_K]
        n = lax.rsqrt(jnp.sum(xh * xh, axis=-1, keepdims=True) + 1e-6)
        if which == 0:
            n = n * (DN_HEAD_K ** -0.5)
        outs.append(xh * n)
    return jnp.concatenate(outs, axis=1)


def _dn_prep_ctx_kernel(q_ref, k_ref, v_ref, w_ref, oq_ref, ok_ref, ov_ref):
    for which, (i_ref, o_ref) in enumerate(((q_ref, oq_ref), (k_ref, ok_ref), (v_ref, ov_ref))):
        x = i_ref[...]
        zero = jnp.zeros((1, x.shape[1]), F32)
        w = w_ref[:, which * DN_KD:(which + 1) * DN_KD]
        o_ref[...] = _dn_norm(_silu(_conv3(x, zero, zero, w)), which)


def _dn_prep_lat_kernel(q_ref, qp_ref, qn_ref, k_ref, kp_ref, kn_ref, v_ref, vp_ref, vn_ref, w_ref,
                        oq_ref, ok_ref, ov_ref):
    g = pl.program_id(1)
    first = g == 0
    last = g == pl.num_programs(1) - 1
    wd = DN_KD
    for which, (i_ref, p_ref, n_ref, o_ref) in enumerate(((q_ref, qp_ref, qn_ref, oq_ref),
                                                         (k_ref, kp_ref, kn_ref, ok_ref),
                                                         (v_ref, vp_ref, vn_ref, ov_ref))):
        rows = i_ref.shape[0]
        cols = [i_ref[:, j, :] for j in range(DN_G)]
        halo_p = jnp.where(first, 0.0, p_ref[7:8, DN_G - 1, :])
        halo_n = jnp.where(last, 0.0, n_ref[0:1, 0, :])
        w = w_ref[:, which * wd:(which + 1) * wd]
        for j in range(DN_G):
            prev_row = halo_p if j == 0 else cols[j - 1][rows - 1:rows, :]
            next_row = halo_n if j == DN_G - 1 else cols[j + 1][0:1, :]
            y = _silu(_conv3(cols[j], prev_row, next_row, w))
            o_ref[:, j * wd:(j + 1) * wd] = _dn_norm(y, which)


def _dn_prep(pq, pk, pv, conv_w, n_batch, seq, ctx_len):
    ntok = pq.shape[0]
    n_lat = n_batch * seq
    wd = DN_KD
    rows = seq // GRID_W
    assert rows == DN_CHUNK and GRID_W % DN_G == 0 and n_lat % ctx_len == 0
    n_ctx = ntok - n_lat
    cin = pl.BlockSpec((ctx_len, wd), lambda b: (n_lat // ctx_len + b, 0))
    cout = pl.BlockSpec((ctx_len, wd), lambda b: (b, 0))
    ctx_out = pl.pallas_call(
        _dn_prep_ctx_kernel,
        grid=(n_batch,),
        in_specs=[cin, cin, cin, pl.BlockSpec((3, 3 * wd), lambda b: (0, 0))],
        out_specs=[cout, cout, cout],
        out_shape=[jax.ShapeDtypeStruct((n_ctx, wd), F32)] * 3,
        compiler_params=_cparams(1),
        name="dn_prep_ctx",
    )(pq, pk, pv, conv_w)
    view = lambda a: a.reshape(ntok // GRID_W, GRID_W, wd)
    n_grp = GRID_W // DN_G
    main = pl.BlockSpec((rows, DN_G, wd), lambda b, g: (b, g, 0))
    prev = pl.BlockSpec((8, DN_G, wd), lambda b, g: (b * (rows // 8) + rows // 8 - 1, jnp.maximum(g - 1, 0), 0))
    nxt = pl.BlockSpec((8, DN_G, wd), lambda b, g: (b * (rows // 8), jnp.minimum(g + 1, n_grp - 1), 0))
    lat_out = pl.pallas_call(
        _dn_prep_lat_kernel,
        grid=(n_batch, n_grp),
        in_specs=[main, prev, nxt] * 3 + [pl.BlockSpec((3, 3 * wd), lambda b, g: (0, 0))],
        out_specs=[pl.BlockSpec((rows, DN_G * wd), lambda b, g: (b, g))] * 3,
        out_shape=[jax.ShapeDtypeStruct((n_lat // GRID_W, GRID_W * wd), F32)] * 3,
        compiler_params=_cparams(2),
        name="dn_prep_lat",
    )(view(pq), view(pq), view(pq), view(pk), view(pk), view(pk), view(pv), view(pv), view(pv), conv_w)
    return lat_out, ctx_out


def _dn_scan_kernel(side_by_side, qf, kf, vf, sf, qb, kb, vb, sb, par_ref, s0_ref, of_ref, ob_ref, s1_ref, st_ref):
    c = DN_CHUNK
    t = pl.program_id(1)

    @pl.when(t == 0)
    def _():
        st_ref[...] = s0_ref[...]

    def sub(ref, s, width):
        return ref[:, s * width:(s + 1) * width] if side_by_side else ref[s * c:(s + 1) * c, :]

    ri, ci = _tri_iota(c)
    dirs = ((qf, kf, vf, sf), (qb, kb, vb, sb))
    chains = [(d, s) for d in range(2) for s in range(SCAN_SUB)]
    masks = [_block_tri(DN_HEADS, c, d == 1) for d in range(2)]
    heads = range(DN_HEADS)
    ops = []
    for d, s in chains:
        q_ref, k_ref, v_ref, s_ref = dirs[d]
        incl_c = (ri >= ci) if d == 0 else (ri <= ci)
        last = c - 1 if d == 0 else 0
        lb, lg = 8 + d * DN_HEADS, 16 + d * DN_HEADS
        sm = sub(s_ref, s, LANES)
        beta_all = _sigmoid(sm)
        gl = _softplus(sm + par_ref[0:1, :]) * par_ref[1:2, :]
        cum = _mm_exact_lhs(incl_c.astype(F32), gl)
        cum_rows = _to_rows(cum, 32)
        beta = jnp.concatenate([beta_all[:, lb + h:lb + h + 1] for h in heads], axis=0)
        cc = jnp.concatenate([cum[:, lg + h:lg + h + 1] for h in heads], axis=0)
        cr = jnp.concatenate([cum_rows[lg + h:lg + h + 1, :] for h in heads], axis=1)
        decay = jnp.where(masks[d][0], jnp.exp(jnp.minimum(cc - cr, 0.0)), 0.0)
        qs = _stack_rows(sub(q_ref, s, DN_KD), DN_HEADS, DN_HEAD_K)
        ks = _stack_rows(sub(k_ref, s, DN_KD), DN_HEADS, DN_HEAD_K)
        vs = _stack_rows(sub(v_ref, s, DN_VD), DN_HEADS, DN_HEAD_V)
        kbeta = ks * beta
        ecc = jnp.exp(cc)
        lcs = [cum[last:last + 1, lg + h:lg + h + 1] for h in heads]
        ops.append(dict(decay=decay, qs=qs, ks=ks, kbeta=kbeta, qe=qs * ecc,
                        rhs=jnp.concatenate([vs * beta, kbeta * ecc], axis=1),
                        kdec=[ks[h * c:(h + 1) * c] * jnp.exp(lcs[h] - cc[h * c:(h + 1) * c]) for h in heads],
                        e_last=[jnp.exp(lc) for lc in lcs]))
    kk = [_mm_nt(o["kbeta"], o["ks"]) for o in ops]
    qk = [_mm_nt(o["qs"], o["ks"]) for o in ops]
    lower = [-jnp.where(masks[d][1], x * o["decay"], 0.0) for (d, _), x, o in zip(chains, kk, ops)]
    attn = [x * o["decay"] for x, o in zip(qk, ops)]
    tinv = _neumann_many(lower, c)
    uw = [_mm(t_, o["rhs"]) for t_, o in zip(tinv, ops)]
    states = [[st_ref[d, h] for h in heads] for d in range(2)]
    for step in range(SCAN_SUB):
        idx = [step, SCAN_SUB + (SCAN_SUB - 1 - step)]
        qw = [[_mm(jnp.concatenate([ops[i]["qe"][h * c:(h + 1) * c], uw[i][h * c:(h + 1) * c, DN_HEAD_V:]], axis=0),
                   states[d][h]) for h in heads] for d, i in enumerate(idx)]
        v_new = [[uw[i][h * c:(h + 1) * c, :DN_HEAD_V] - qw[d][h][c:] for h in heads] for d, i in enumerate(idx)]
        o = [jnp.concatenate([qw[d][h][:c] for h in heads], axis=0)
             + _mm(attn[i], jnp.concatenate(v_new[d], axis=0)) for d, i in enumerate(idx)]
        states = [[ops[i]["e_last"][h] * states[d][h] + _mm_tn(ops[i]["kdec"][h], v_new[d][h]) for h in heads]
                  for d, i in enumerate(idx)]
        for d, (i, o_ref) in enumerate(zip(idx, (of_ref, ob_ref))):
            s = chains[i][1]
            o_cols = jnp.concatenate([o[d][h * c:(h + 1) * c] for h in heads], axis=1)
            if side_by_side:
                o_ref[:, s * DN_VD:(s + 1) * DN_VD] = o_cols
            else:
                o_ref[s * c:(s + 1) * c, :] = o_cols
    for d in range(2):
        for h in heads:
            st_ref[d, h] = states[d][h]

    @pl.when(t == pl.num_programs(1) - 1)
    def _():
        s1_ref[...] = st_ref[...]


def _dn_scan(lat_qkv, ctx_qkv, small, par, n_batch, seq, ctx_len):
    ntok = small.shape[0]
    n_lat = n_batch * seq
    c = DN_CHUNK
    wd = DN_KD
    st_shape = (n_batch, 2, DN_HEADS, DN_HEAD_K, DN_HEAD_V)
    st_spec = pl.BlockSpec((None, 2, DN_HEADS, DN_HEAD_K, DN_HEAD_V), lambda b, t: (b, 0, 0, 0, 0))
    par_spec = pl.BlockSpec((8, LANES), lambda b, t: (0, 0))

    def run(arrs, sm, s0, nsteps, side_by_side, blk_of, sm_blk_of, name):
        fw = lambda f: (lambda b, t: f(b, t))
        bw = lambda f: (lambda b, t: f(b, nsteps - 1 - t))
        shp = lambda w: (c, SCAN_SUB * w) if side_by_side else (SCAN_SUB * c, w)
        ins = lambda o: [pl.BlockSpec(shp(wd), o(blk_of))] * 3 + [pl.BlockSpec(shp(LANES), o(sm_blk_of))]
        return pl.pallas_call(
            functools.partial(_dn_scan_kernel, side_by_side),
            grid=(n_batch, nsteps),
            in_specs=ins(fw) + ins(bw) + [par_spec, st_spec],
            out_specs=[pl.BlockSpec(shp(wd), fw(blk_of)), pl.BlockSpec(shp(wd), bw(blk_of)), st_spec],
            out_shape=[jax.ShapeDtypeStruct(arrs[0].shape, F32)] * 2 + [jax.ShapeDtypeStruct(st_shape, F32)],
            scratch_shapes=[pltpu.VMEM((2, DN_HEADS, DN_HEAD_K, DN_HEAD_V), F32)],
            compiler_params=_cparams(2),
            name=name,
        )(*arrs, sm, *arrs, sm, par, s0)

    ctx_steps = ctx_len // (c * SCAN_SUB)
    ctx_base = n_lat // (c * SCAN_SUB)
    ocf, ocb, s_ctx = run(ctx_qkv, small, jnp.zeros(st_shape, F32), ctx_steps, False,
                          lambda b, j: (b * ctx_steps + j, 0),
                          lambda b, j: (ctx_base + b * ctx_steps + j, 0), "dn_scan_ctx")
    olf, olb, _ = run(lat_qkv, small.reshape(ntok // GRID_W, GRID_W * LANES), s_ctx, GRID_W // SCAN_SUB, True,
                      lambda b, j: (b, j), lambda b, j: (b, j), "dn_scan_lat")
    return olf.reshape(n_lat, wd), olb.reshape(n_lat, wd), ocf, ocb


def _rw_post(x, shifted, mu_ref, vec_ref, a2_ref, g2_ref, w2_ref, o_ref):
    pm = x + (shifted - x) * mu_ref[...]
    r = pm[:, 0:RW_DIM]
    k = pm[:, RW_DIM:2 * RW_DIM]
    v = pm[:, 2 * RW_DIM:3 * RW_DIM]
    base = 3 * RW_DIM
    ad = pm[:, base + 2 * RW_DECAY_LORA:base + 2 * RW_DECAY_LORA + RW_ICLR_LORA]
    gd = pm[:, base + 2 * RW_DECAY_LORA + RW_ICLR_LORA:RW_COLS]
    a0, k_k, k_a, r_k = vec_ref[0:1, :], vec_ref[1:2, :], vec_ref[2:3, :], vec_ref[3:4, :]
    a = _sigmoid(a0 + _mm(ad, a2_ref[...]))
    g = _mm(_sigmoid(gd), g2_ref[...])
    kx = k * k_k
    kk = kx * lax.rsqrt(_head_sum(kx * kx, RW_HEAD) + 1e-6)
    k2 = k * (1.0 + (a - 1.0) * k_a)
    bonus = _head_sum(r * k2 * r_k, RW_HEAD) * v
    cols = [r, k2, v, -kk, kk * a]
    for d in range(2):
        wd_d = pm[:, base + d * RW_DECAY_LORA:base + (d + 1) * RW_DECAY_LORA]
        pre = vec_ref[4 + d:5 + d, :] + _mm(jnp.tanh(wd_d), w2_ref[d])
        cols.append(-jnp.exp(-_softplus(-pre) - 0.5))
    cols += [bonus, g]
    for j, cval in enumerate(cols):
        o_ref[:, j * RW_DIM:(j + 1) * RW_DIM] = cval


def _rw_prep_kernel(tiles_per_batch, n_lat_tiles, x_ref, u_ref, d_ref, mu_ref, vec_ref, a2_ref, g2_ref, w2_ref, o_ref):
    i = pl.program_id(0)
    x = x_ref[...]
    rows = x.shape[0]
    ridx = lax.broadcasted_iota(jnp.int32, x.shape, 0)
    lane = lax.broadcasted_iota(jnp.int32, x.shape, 1)
    before = _shift_rows(x, 1)
    after = _shift_rows(x, -1)
    col = ridx % GRID_W
    left = jnp.where(col == 0, 0.0, before)
    right = jnp.where(col == GRID_W - 1, 0.0, after)
    top = jnp.where(i % tiles_per_batch == 0, 0.0, u_ref[...])
    bot = jnp.where(i % tiles_per_batch == tiles_per_batch - 1, 0.0, d_ref[...])
    up = jnp.concatenate([top, x[:rows - GRID_W, :]], axis=0)
    down = jnp.concatenate([x[GRID_W:, :], bot], axis=0)
    sel = lane % 4
    shifted_lat = jnp.where(sel == 0, left, jnp.where(sel == 1, right, jnp.where(sel == 2, up, down)))
    prev = jnp.where(ridx == 0, 0.0, before)
    nxt = jnp.where(ridx == rows - 1, 0.0, after)
    shifted_ctx = jnp.where(lane % 2 == 0, prev, nxt)
    shifted = jnp.where(i < n_lat_tiles, shifted_lat, shifted_ctx)
    _rw_post(x, shifted, mu_ref, vec_ref, a2_ref, g2_ref, w2_ref, o_ref)


def _rw_prep(p_rw, mu, vec, a2, g2, w2, n_batch, seq, ctx_len):
    ntok, wd = p_rw.shape
    n_lat = n_batch * seq
    wo = RW_NOUT * RW_DIM
    tm = RW_TM
    assert ctx_len == tm and seq % tm == 0 and tm % GRID_W == 0
    per = tm // GRID_W
    last_blk = ntok // GRID_W - 1
    consts = [pl.BlockSpec(a.shape, (lambda i, nd=a.ndim: (0,) * nd)) for a in (mu, vec, a2, g2, w2)]
    return pl.pallas_call(
        functools.partial(_rw_prep_kernel, seq // tm, n_lat // tm),
        grid=(ntok // tm,),
        in_specs=[pl.BlockSpec((tm, wd), lambda i: (i, 0)),
                  pl.BlockSpec((GRID_W, wd), lambda i: (jnp.maximum(i * per - 1, 0), 0)),
                  pl.BlockSpec((GRID_W, wd), lambda i: (jnp.minimum((i + 1) * per, last_blk), 0))] + consts,
        out_specs=pl.BlockSpec((tm, wo), lambda i: (i, 0)),
        out_shape=jax.ShapeDtypeStruct((ntok, wo), F32),
        compiler_params=_cparams(1),
        name="rw_prep",
    )(p_rw, p_rw, p_rw, mu, vec, a2, g2, w2)


def _rw_scan_kernel(rf, kf, vf, af, bf, wf, rb, kb, vb, ab, bb, wb, yf_ref, yb_ref, st_ref):
    c = RW_CHUNK

    @pl.when(pl.program_id(1) == 0)
    def _():
        st_ref[...] = jnp.zeros_like(st_ref)

    ri, ci = _tri_iota(c)
    nr = RW_HEADS * c
    head_mask = (lax.broadcasted_iota(jnp.int32, (nr, RW_DIM), 0) // c
                 == lax.broadcasted_iota(jnp.int32, (nr, RW_DIM), 1) // RW_HEAD)

    def stacked(x):
        xb = x.astype(BF16)
        return jnp.where(head_mask, jnp.concatenate([xb] * RW_HEADS, axis=0), jnp.zeros((), BF16))

    dirs = ((rf, kf, vf, af, bf, wf), (rb, kb, vb, ab, bb, wb))
    chains = [(d, s) for d in range(2) for s in range(SCAN_SUB)]
    masks = [_block_tri(RW_HEADS, c, d == 1) for d in range(2)]
    ops = []
    for d, s in chains:
        incl_c = (ri >= ci) if d == 0 else (ri <= ci)
        last = c - 1 if d == 0 else 0
        r, k, v, a, b, lw = [ref[s * c:(s + 1) * c, :] for ref in dirs[d]]
        cum = _mm_exact_lhs(incl_c.astype(F32), lw)
        lc = cum[last:last + 1, :]
        e_neg = jnp.exp(-cum)
        e_rel = jnp.exp(lc - cum)
        ops.append(dict(rt=stacked(r * jnp.exp(cum)), at=stacked(a * jnp.exp(cum - lw)),
                        bt=stacked(b * e_neg), kt=stacked(k * e_neg), bl=stacked(b * e_rel), kl=stacked(k * e_rel),
                        vs=stacked(v), e_last=jnp.exp(lc)))
    raw = [(_mm_nt(o["at"], o["bt"]), _mm_nt(o["at"], o["kt"]), _mm_nt(o["rt"], o["bt"]), _mm_nt(o["rt"], o["kt"]))
           for o in ops]
    aab = [jnp.where(masks[d][1], x[0], 0.0) for (d, _), x in zip(chains, raw)]
    aak = [jnp.where(masks[d][1], x[1], 0.0) for (d, _), x in zip(chains, raw)]
    arb = [jnp.where(masks[d][0], x[2], 0.0) for (d, _), x in zip(chains, raw)]
    ark = [jnp.where(masks[d][0], x[3], 0.0) for (d, _), x in zip(chains, raw)]
    aak_v = [_mm(m, o["vs"]) for m, o in zip(aak, ops)]
    ark_v = [_mm(m, o["vs"]) for m, o in zip(ark, ops)]
    tinv = _neumann_many(aab, c)
    state = [st_ref[0], st_ref[1]]
    for step in range(SCAN_SUB):
        idx = [step, SCAN_SUB + (SCAN_SUB - 1 - step)]
        a_s = [_mm_nt(ops[i]["at"], state[d]) for d, i in enumerate(idx)]
        r_s = [_mm_nt(ops[i]["rt"], state[d]) for d, i in enumerate(idx)]
        u = [_mm(tinv[i], a_s[d] + aak_v[i]) for d, i in enumerate(idx)]
        y = [r_s[d] + _mm(arb[i], u[d]) + ark_v[i] for d, i in enumerate(idx)]
        state = [state[d] * ops[i]["e_last"] + _mm_tn(u[d], ops[i]["bl"]) + _mm_tn(ops[i]["vs"], ops[i]["kl"])
                 for d, i in enumerate(idx)]
        for d, (i, y_ref) in enumerate(zip(idx, (yf_ref, yb_ref))):
            s = chains[i][1]
            y_ref[s * c:(s + 1) * c, :] = functools.reduce(
                lambda p, q: p + q, [y[d][h * c:(h + 1) * c] for h in range(RW_HEADS)])
    st_ref[0] = state[0]
    st_ref[1] = state[1]


def _rw_scan(rwp, n_batch, seq, ctx_len):
    ntok = rwp.shape[0]
    c = RW_CHUNK * SCAN_SUB
    fwd, bwd, steps = _chunk_maps(n_batch * seq, seq, ctx_len, c)

    def specs(m, wcol):
        return [pl.BlockSpec((c, RW_DIM), lambda b, t, j=j: (m(b, t), j)) for j in (0, 1, 2, 3, 4, wcol)]

    return pl.pallas_call(
        _rw_scan_kernel,
        grid=(n_batch, steps),
        in_specs=specs(fwd, 5) + specs(bwd, 6),
        out_specs=[pl.BlockSpec((c, RW_DIM), lambda b, t: (fwd(b, t), 0)),
                   pl.BlockSpec((c, RW_DIM), lambda b, t: (bwd(b, t), 0))],
        out_shape=[jax.ShapeDtypeStruct((ntok, RW_DIM), F32)] * 2,
        scratch_shapes=[pltpu.VMEM((2, RW_DIM, RW_DIM), F32)],
        compiler_params=_cparams(2),
        name="rw_scan",
    )(*([rwp] * 12))


def _outproj_kernel(n_lat_tiles, x_ref, ymf, ymb, xs_ref, z_ref, odf, odb, ocf, ocb, gate_ref, yrf, yrb, bonus_ref,
                    g_ref, vec_ref, dnw_ref, wout_ref, gmsa_ref, n2w_ref, sh_ref, sc_ref, wr_ref, br_ref,
                    xo_ref, h2_ref, route_ref, cnt_ref, run_ref):
    tm = x_ref.shape[0]

    @pl.when(pl.program_id(0) == 0)
    def _():
        run_ref[...] = jnp.zeros_like(run_ref)

    ym = (xs_ref[...] * vec_ref[0:1, :] + ymf[...] + ymb[...]) * _silu(z_ref[...])
    gw = M_INNER // M_GROUPS
    parts = []
    for g in range(M_GROUPS):
        yg = ym[:, g * gw:(g + 1) * gw]
        parts.append(yg * lax.rsqrt(jnp.mean(yg * yg, axis=-1, keepdims=True) + 1e-5))
    mix_m = jnp.concatenate(parts, axis=1) * vec_ref[1:2, :]
    od = jnp.where(pl.program_id(0) < n_lat_tiles, odf[...] + odb[...], ocf[...] + ocb[...])
    parts = []
    for h in range(DN_HEADS):
        oh = od[:, h * DN_HEAD_V:(h + 1) * DN_HEAD_V]
        parts.append(oh * lax.rsqrt(jnp.mean(oh * oh, axis=-1, keepdims=True) + 1e-6))
    mix_d = jnp.concatenate(parts, axis=1) * dnw_ref[...] * _silu(gate_ref[...])
    yr = yrf[...] + yrb[...]
    mean = _head_sum(yr, RW_HEAD) * (1.0 / RW_HEAD)
    cen = yr - mean
    var = _head_sum(cen * cen, RW_HEAD) * (1.0 / RW_HEAD)
    mix_r = ((cen * lax.rsqrt(var + RW_LN_EPS)) * vec_ref[2:3, :] + vec_ref[3:4, :] + bonus_ref[...]) * g_ref[...]

    attn = (_mm(mix_m, wout_ref[0:M_INNER, :]) + _mm(mix_d, wout_ref[M_INNER:M_INNER + DN_VD, :])
            + _mm(mix_r, wout_ref[M_INNER + DN_VD:, :]))
    xn = x_ref[...] + gmsa_ref[...] * attn
    xo_ref[...] = xn
    h2 = xn * lax.rsqrt(jnp.mean(xn * xn, axis=-1, keepdims=True) + NORM_EPS) * n2w_ref[...]
    h2 = h2 * (1.0 + sc_ref[...]) + sh_ref[...]
    h2_ref[...] = h2

    logits = _mm3(h2, wr_ref[...]) + br_ref[...]
    lane = lax.broadcasted_iota(jnp.int32, (tm, LANES), 1)
    neg = jnp.float32(-jnp.inf)
    lg = jnp.where(lane < N_EXPERTS, logits, neg)
    tops, idxs = [], []
    onehot = jnp.zeros((tm, LANES), F32)
    for _ in range(TOP_K):
        m = jnp.max(lg, axis=-1, keepdims=True)
        idx = jnp.min(jnp.where(lg == m, lane, LANES), axis=-1, keepdims=True)
        sel = lane == idx
        tops.append(m)
        idxs.append(idx)
        lg = jnp.where(sel, neg, lg)
        onehot = onehot + sel.astype(F32)
    ri, ci = _tri_iota(tm)
    before = jnp.dot((ri > ci).astype(BF16), onehot.astype(BF16), preferred_element_type=F32) + run_ref[0:1, :]
    run_ref[0:1, :] = run_ref[0:1, :] + jnp.sum(onehot, axis=0, keepdims=True)
    exps = [jnp.exp(m - tops[0]) for m in tops]
    denom = exps[0] + exps[1] + exps[2] + exps[3]
    route = jnp.zeros((tm, LANES), F32)
    for k in range(TOP_K):
        rank = jnp.sum(jnp.where(lane == idxs[k], before, 0.0), axis=-1, keepdims=True)
        route = jnp.where(lane == k, idxs[k].astype(F32), route)
        route = jnp.where(lane == TOP_K + k, exps[k] / denom, route)
        route = jnp.where(lane == 2 * TOP_K + k, rank, route)
    route_ref[...] = route
    cnt_ref[...] = run_ref[...]


def _outproj(x_all, ssd_y, xbc, p_m, dn_o, p_gate, rw_y, rwp, vec, dnw, w_out, mod3, layer, norm2_w,
             w_router, b_router, n_rows, seq, n_batch):
    d = x_all.shape[1]
    tm = OUT_TM
    row = lambda wd, col=0: pl.BlockSpec((tm, wd), lambda i, col=col: (i, col))
    const = lambda a: pl.BlockSpec(a.shape, lambda i: (0,) * a.ndim)
    n2 = norm2_w.reshape(1, d)
    n_lat_tiles = n_batch * seq // tm
    lat_row = pl.BlockSpec((tm, DN_VD), lambda i: (jnp.minimum(i, n_lat_tiles - 1), 0))
    ctx_row = pl.BlockSpec((tm, DN_VD), lambda i: (jnp.maximum(i - n_lat_tiles, 0), 0))
    ins = [x_all, ssd_y[0], ssd_y[1], xbc, p_m, dn_o[0], dn_o[1], dn_o[2], dn_o[3], p_gate, rw_y[0], rw_y[1],
           rwp, rwp, vec, dnw, w_out, mod3, n2, mod3, mod3, w_router, b_router]
    specs = [row(d), row(M_INNER), row(M_INNER), row(M_INNER, 0), row(M_INNER, M_CONV_DIM // M_INNER),
             lat_row, lat_row, ctx_row, ctx_row, row(DN_VD), row(RW_DIM), row(RW_DIM), row(RW_DIM, 7),
             row(RW_DIM, 8), const(vec), const(dnw), const(w_out), _mod_spec(layer, 2, tm, seq, n_batch),
             const(n2), _mod_spec(layer, 3, tm, seq, n_batch), _mod_spec(layer, 4, tm, seq, n_batch),
             const(w_router), const(b_router)]
    return pl.pallas_call(
        functools.partial(_outproj_kernel, n_lat_tiles),
        grid=(n_rows // tm,),
        in_specs=specs,
        out_specs=[row(d), row(d), row(LANES), pl.BlockSpec((8, LANES), lambda i: (0, 0))],
        out_shape=[jax.ShapeDtypeStruct((n_rows, d), F32), jax.ShapeDtypeStruct((n_rows, d), F32),
                   jax.ShapeDtypeStruct((n_rows, LANES), F32), jax.ShapeDtypeStruct((8, LANES), F32)],
        scratch_shapes=[pltpu.VMEM((8, LANES), F32)],
        compiler_params=_cparams(1),
        name="finish_outproj_router",
    )(*ins)


def _dispatch_kernel(dest_ref, h_ref, zeros_ref, xs_ref, sem):
    del zeros_ref
    tm = h_ref.shape[0]

    def copy(i, k):
        return pltpu.make_async_copy(h_ref.at[pl.ds(i, 1)], xs_ref.at[pl.ds(dest_ref[0, i * TOP_K + k], 1)], sem)

    def start(i, carry):
        for k in range(TOP_K):
            copy(i, k).start(priority=k % 2)
        return carry

    def wait(i, carry):
        for k in range(TOP_K):
            copy(i, k).wait()
        return carry

    lax.fori_loop(0, tm, start, 0)
    lax.fori_loop(0, tm, wait, 0)


def _dispatch(h2, dest, n_slots):
    n, d = h2.shape
    tm = MOE_TM
    return pl.pallas_call(
        _dispatch_kernel,
        grid=(n // tm,),
        in_specs=[pl.BlockSpec((None, 1, tm * TOP_K), lambda i: (i, 0, 0), memory_space=pltpu.SMEM),
                  pl.BlockSpec((tm, d), lambda i: (i, 0)),
                  pl.BlockSpec(memory_space=pl.ANY)],
        out_specs=pl.BlockSpec(memory_space=pl.ANY),
        out_shape=jax.ShapeDtypeStruct((n_slots, d), F32),
        scratch_shapes=[pltpu.SemaphoreType.DMA(())],
        input_output_aliases={2: 0},
        compiler_params=_cparams(1),
        name="moe_dispatch",
    )(dest.reshape(n // tm, 1, tm * TOP_K), h2, jnp.zeros((n_slots, d), F32))


def _moe_kernel(be_ref, nv_ref, x_ref, wgu_ref, bgu_ref, wdn_ref, bdn_ref, y_ref, wgu_bf, wdn_bf):
    j = pl.program_id(0)

    @pl.when(j < nv_ref[0])
    def _():
        changed = jnp.logical_or(j == 0, be_ref[j] != be_ref[jnp.maximum(j - 1, 0)])

        @pl.when(changed)
        def _():
            wgu_bf[...] = wgu_ref[...].astype(BF16)
            wdn_bf[...] = wdn_ref[...].astype(BF16)

        gu = jnp.dot(x_ref[...].astype(BF16), wgu_bf[...], preferred_element_type=F32) + bgu_ref[...]
        g_ = jnp.minimum(gu[:, :D_FF], SWIGLU_LIMIT)
        u_ = jnp.clip(gu[:, D_FF:], -SWIGLU_LIMIT, SWIGLU_LIMIT)
        act = g_ * _sigmoid(SWIGLU_ALPHA * g_) * (u_ + 1.0)
        y_ref[...] = jnp.dot(act.astype(BF16), wdn_bf[...], preferred_element_type=F32) + bdn_ref[...]

    @pl.when(j >= nv_ref[0])
    def _():
        y_ref[...] = jnp.zeros_like(y_ref)


def _moe(xs, block_expert, n_valid, layer, w_gate_up, b_gate_up, w_down, b_down):
    n_slots, d = xs.shape
    bm = MOE_BLOCK
    n_layers, n_exp, _, f2 = w_gate_up.shape
    blk = lambda j, be, nv: (jnp.minimum(j, nv[0] - 1), 0)
    grid_spec = pltpu.PrefetchScalarGridSpec(
        num_scalar_prefetch=2,
        grid=(n_slots // bm,),
        in_specs=[pl.BlockSpec((bm, d), blk),
                  pl.BlockSpec((None, None, d, f2), lambda j, be, nv: (layer, be[j], 0, 0)),
                  pl.BlockSpec((None, None, 1, f2), lambda j, be, nv: (layer, be[j], 0, 0)),
                  pl.BlockSpec((None, None, f2 // 2, d), lambda j, be, nv: (layer, be[j], 0, 0)),
                  pl.BlockSpec((None, None, 1, d), lambda j, be, nv: (layer, be[j], 0, 0))],
        out_specs=pl.BlockSpec((bm, d), lambda j, be, nv: (j, 0)),
        scratch_shapes=[pltpu.VMEM((d, f2), BF16), pltpu.VMEM((f2 // 2, d), BF16)],
    )
    return pl.pallas_call(
        _moe_kernel,
        grid_spec=grid_spec,
        out_shape=jax.ShapeDtypeStruct((n_slots, d), F32),
        compiler_params=_cparams(1),
        name="moe_experts",
    )(block_expert, n_valid, xs, w_gate_up, b_gate_up.reshape(n_layers, n_exp, 1, f2), w_down,
      b_down.reshape(n_layers, n_exp, 1, d))


def _combine_kernel(final_norm, dest_ref, x_ref, route_ref, g_ref, nf_ref, y_ref, o_ref, buf, sem):
    tm = x_ref.shape[0]

    def copy(i, k):
        return pltpu.make_async_copy(y_ref.at[pl.ds(dest_ref[0, i * TOP_K + k], 1)], buf.at[k, pl.ds(i, 1)], sem)

    def start(i, carry):
        for k in range(TOP_K):
            copy(i, k).start(priority=k % 2)
        return carry

    def wait(i, carry):
        for k in range(TOP_K):
            copy(i, k).wait()
        return carry

    lax.fori_loop(0, tm, start, 0)
    lax.fori_loop(0, tm, wait, 0)
    route = route_ref[...]
    acc = jnp.zeros(x_ref.shape, F32)
    for k in range(TOP_K):
        acc = acc + route[:, TOP_K + k:TOP_K + k + 1] * buf[k]
    out = x_ref[...] + g_ref[...] * acc
    if final_norm:
        out = out * lax.rsqrt(jnp.mean(out * out, axis=-1, keepdims=True) + NORM_EPS) * nf_ref[...]
    o_ref[...] = out


def _combine(x_new, route, dest, y_slots, mod3, layer, norm_f_w, final_norm, seq, n_batch):
    n, d = x_new.shape
    tm = MOE_TM
    row = lambda wd: pl.BlockSpec((tm, wd), lambda i: (i, 0))
    return pl.pallas_call(
        functools.partial(_combine_kernel, final_norm),
        grid=(n // tm,),
        in_specs=[pl.BlockSpec((None, 1, tm * TOP_K), lambda i: (i, 0, 0), memory_space=pltpu.SMEM),
                  row(d), row(LANES), _mod_spec(layer, 5, tm, seq, n_batch),
                  pl.BlockSpec((1, d), lambda i: (0, 0)),
                  pl.BlockSpec(memory_space=pl.ANY)],
        out_specs=row(d),
        out_shape=jax.ShapeDtypeStruct((n, d), F32),
        scratch_shapes=[pltpu.VMEM((TOP_K, tm, d), F32), pltpu.SemaphoreType.DMA(())],
        compiler_params=_cparams(1),
        name="moe_combine",
    )(dest.reshape(n // tm, 1, tm * TOP_K), x_new, route, mod3, norm_f_w.reshape(1, d), y_slots)


def _routing_tables(route, counts, n_slots):
    bm = MOE_BLOCK
    top_e = route[:, :TOP_K].astype(jnp.int32)
    rank = route[:, 2 * TOP_K:3 * TOP_K].astype(jnp.int32)
    cnt = counts[0, :N_EXPERTS].astype(jnp.int32)
    padded = (cnt + bm - 1) // bm * bm
    ends = jnp.cumsum(padded)
    dest = (ends - padded)[top_e] + rank
    n_blocks = n_slots // bm
    n_valid = ends[-1] // bm
    blk = jnp.minimum(jnp.arange(n_blocks, dtype=jnp.int32), n_valid - 1)
    block_expert = jnp.sum((blk[:, None] >= (ends // bm)[None, :]).astype(jnp.int32), axis=1)
    block_expert = jnp.minimum(block_expert, N_EXPERTS - 1)
    return dest.reshape(-1), block_expert, n_valid.reshape(1).astype(jnp.int32)


def _pad_cols(a, width):
    return jnp.pad(a, [(0, 0)] * (a.ndim - 1) + [(0, width - a.shape[-1])])


def kernel(x, c, ctx, c_ctx, w_mod, b_mod, norm1_w, norm2_w, w_in, w_out, m_conv_w, m_conv_b, m_dt_bias, m_a_log, m_d, m_norm_w, dn_conv_w, dn_dt_bias, dn_a_log, dn_norm_w, rw_mu, rw_w0, rw_w2, rw_a0, rw_a2, rw_g2, rw_k_k, rw_k_a, rw_r_k, rw_ln_w, rw_ln_b, w_router, b_router, w_gate_up, b_gate_up, w_down, b_down, norm_f_w):
    n_batch, seq, d = x.shape
    ctx_len = ctx.shape[1]
    depth = w_mod.shape[0]
    n_lat, n_ctx = n_batch * seq, n_batch * ctx_len
    assert d == D_MODEL and seq // GRID_W == DN_CHUNK and n_batch + 1 <= 8

    x_all = jnp.concatenate([x.reshape(n_lat, d), ctx.reshape(n_ctx, d)], axis=0)
    c8 = jnp.zeros((8, d), F32).at[:n_batch].set(c).at[n_batch].set(c_ctx)
    mod3 = _modulation(c8, w_mod, b_mod).reshape(depth * 6 * 8, 1, d)

    out = None
    for i in range(depth):
        last = i == depth - 1
        wi = w_in[i]
        mo, do, ro = 0, M_COLS, M_COLS + DN_COLS
        w_m = jnp.concatenate([wi[:, mo + M_INNER:mo + M_INNER + M_CONV_DIM], wi[:, mo:mo + M_INNER]], axis=1)
        w_q = wi[:, do:do + DN_KD]
        w_k = wi[:, do + DN_KD:do + 2 * DN_KD]
        w_v = wi[:, do + 2 * DN_KD:do + DN_CONV_DIM]
        w_g = wi[:, do + DN_CONV_DIM:do + DN_CONV_DIM + DN_VD]
        w_r = _pad_cols(wi[:, ro:ro + RW_COLS], RW_COLS_PAD)
        w_s = _pad_cols(jnp.concatenate([wi[:, mo + M_INNER + M_CONV_DIM:mo + M_COLS],
                                         wi[:, do + DN_CONV_DIM + DN_VD:do + DN_COLS]], axis=1), LANES)
        weights = [w.astype(BF16) for w in (w_m, w_q, w_k, w_v, w_g, w_r, w_s)]
        p_m, p_q, p_k, p_v, p_gate, p_rw, small = _inproj(x_all, norm1_w[i], mod3, i, weights, seq, n_batch)

        xbc = _ssd_prep(p_m, m_conv_w[i], m_conv_b[i], seq, n_lat, ctx_len)
        ssd_par = jnp.zeros((8, LANES), F32)
        ssd_par = ssd_par.at[0, :2 * M_HEADS].set(m_dt_bias[i].reshape(-1))
        ssd_par = ssd_par.at[1, :2 * M_HEADS].set(-jnp.exp(m_a_log[i].astype(F32)).reshape(-1))
        ssd_y = _ssd_scan(xbc, small, ssd_par, n_batch, seq, ctx_len)

        lat_qkv, ctx_qkv = _dn_prep(p_q, p_k, p_v, dn_conv_w[i], n_batch, seq, ctx_len)
        dn_par = jnp.zeros((8, LANES), F32)
        dn_par = dn_par.at[0, 16:16 + 2 * DN_HEADS].set(dn_dt_bias[i].reshape(-1))
        dn_par = dn_par.at[1, 16:16 + 2 * DN_HEADS].set(-jnp.exp(dn_a_log[i].astype(F32)).reshape(-1))
        dn_o = _dn_scan(lat_qkv, ctx_qkv, small, dn_par, n_batch, seq, ctx_len)

        rw_vec = jnp.stack([rw_a0[i], rw_k_k[i], rw_k_a[i], rw_r_k[i].reshape(-1), rw_w0[i, 0], rw_w0[i, 1],
                            jnp.zeros((RW_DIM,), F32), jnp.zeros((RW_DIM,), F32)])
        rwp = _rw_prep(p_rw, _pad_cols(rw_mu[i], RW_COLS_PAD).reshape(1, -1), rw_vec, rw_a2[i], rw_g2[i], rw_w2[i],
                       n_batch, seq, ctx_len)
        rw_y = _rw_scan(rwp, n_batch, seq, ctx_len)

        n_rows = n_lat if last else n_lat + n_ctx
        fin_vec = jnp.stack([jnp.repeat(m_d[i], M_HEAD_DIM), m_norm_w[i], rw_ln_w[i], rw_ln_b[i]]
                            + [jnp.zeros((RW_DIM,), F32)] * 4)
        dnw = jnp.tile(dn_norm_w[i], DN_HEADS).reshape(1, DN_VD)
        x_new, h2, route, counts = _outproj(
            x_all, ssd_y, xbc, p_m, dn_o, p_gate, rw_y, rwp, fin_vec, dnw, w_out[i].astype(BF16), mod3, i,
            norm2_w[i], _pad_cols(w_router[i], LANES), _pad_cols(b_router[i].reshape(1, -1), LANES),
            n_rows, seq, n_batch)

        n_slots = (n_rows * TOP_K // MOE_BLOCK + N_EXPERTS) * MOE_BLOCK
        dest, block_expert, n_valid = _routing_tables(route, counts, n_slots)
        xs = _dispatch(h2, dest, n_slots)
        y_slots = _moe(xs, block_expert, n_valid, i, w_gate_up, b_gate_up, w_down, b_down)
        res = _combine(x_new, route, dest, y_slots, mod3, i, norm_f_w, last, seq, n_batch)
        if last:
            out = res
        else:
            x_all = res
    return out.reshape(n_batch, seq, d)
```

```python
import functools
import math

import jax
import jax.numpy as jnp
from jax import lax
from jax.experimental import pallas as pl
from jax.experimental.pallas import tpu as pltpu

F32 = jnp.float32
BF16 = jnp.bfloat16

D_MODEL = 1024
GRID_W = 64
NORM_EPS = 1e-6

M_HEADS, M_HEAD_DIM, M_STATE, M_GROUPS = 4, 64, 128, 2
M_INNER = M_HEADS * M_HEAD_DIM
M_CONV_DIM = M_INNER + 2 * M_GROUPS * M_STATE
M_COLS = M_INNER + M_CONV_DIM + 2 * M_HEADS
M_CHUNK = 128

DN_HEADS, DN_HEAD_K, DN_HEAD_V = 4, 128, 128
DN_KD = DN_HEADS * DN_HEAD_K
DN_VD = DN_HEADS * DN_HEAD_V
DN_CONV_DIM = 2 * DN_KD + DN_VD
DN_COLS = DN_CONV_DIM + DN_VD + 4 * DN_HEADS
DN_CHUNK = 64

RW_HEADS, RW_HEAD = 4, 64
RW_DIM = RW_HEADS * RW_HEAD
RW_DECAY_LORA, RW_ICLR_LORA, RW_GATE_LORA = 64, 64, 128
RW_COLS = 3 * RW_DIM + 2 * RW_DECAY_LORA + RW_ICLR_LORA + RW_GATE_LORA
RW_COLS_PAD = 1152
RW_LN_EPS = RW_HEAD * 1e-5
RW_CHUNK = 64
RW_NOUT = 9

N_EXPERTS, TOP_K = 32, 4
D_FF = 1024
SWIGLU_ALPHA, SWIGLU_LIMIT = 1.702, 7.0
MOE_BLOCK = 512

LANES = 128
VMEM_LIMIT = 56 * 1024 * 1024

IN_TM = 512
CONV_TM = 256
DN_G = 8
RW_TM = 256
OUT_TM = 256
MOE_TM = 256
SCAN_SUB = 4


def _cparams(n_axes):
    return pltpu.CompilerParams(dimension_semantics=("arbitrary",) * n_axes, vmem_limit_bytes=VMEM_LIMIT)


def _sigmoid(x):
    return 1.0 / (1.0 + jnp.exp(-x))


def _silu(x):
    return x * _sigmoid(x)


def _softplus(x):
    return jnp.maximum(x, 0.0) + jnp.log(1.0 + jnp.exp(-jnp.abs(x)))


def _mm(a, b):
    return jnp.dot(a.astype(BF16), b.astype(BF16), preferred_element_type=F32)


def _mm_nt(a, b):
    return lax.dot_general(a.astype(BF16), b.astype(BF16), (((1,), (1,)), ((), ())), preferred_element_type=F32)


def _mm_tn(a, b):
    return lax.dot_general(a.astype(BF16), b.astype(BF16), (((0,), (0,)), ((), ())), preferred_element_type=F32)


def _split3(x):
    hi = x.astype(BF16)
    r1 = x - hi.astype(F32)
    mid = r1.astype(BF16)
    lo = (r1 - mid.astype(F32)).astype(BF16)
    return hi, mid, lo


def _mm3(a, b):
    ah, am, _ = _split3(a)
    bh, bm, _ = _split3(b)
    dot = functools.partial(jnp.dot, preferred_element_type=F32)
    return dot(ah, bh) + (dot(am, bh) + dot(ah, bm))


def _mm_exact_lhs(a01, x):
    a = a01.astype(BF16)
    dot = functools.partial(jnp.dot, preferred_element_type=F32)
    hi, mid, lo = _split3(x)
    return dot(a, hi) + (dot(a, mid) + dot(a, lo))


def _mm_exact_rhs(x, b01):
    b = b01.astype(BF16)
    dot = functools.partial(jnp.dot, preferred_element_type=F32)
    hi, mid, lo = _split3(x)
    return dot(hi, b) + (dot(mid, b) + dot(lo, b))


def _to_rows(x, n_rows):
    sel = (lax.broadcasted_iota(jnp.int32, (n_rows, LANES), 0)
           == lax.broadcasted_iota(jnp.int32, (n_rows, LANES), 1)).astype(BF16)
    nt = functools.partial(lax.dot_general, dimension_numbers=(((1,), (1,)), ((), ())), preferred_element_type=F32)
    hi, mid, lo = _split3(x)
    return nt(sel, hi) + (nt(sel, mid) + nt(sel, lo))


def _tri_iota(c):
    return lax.broadcasted_iota(jnp.int32, (c, c), 0), lax.broadcasted_iota(jnp.int32, (c, c), 1)


def _neumann_many(ms, nil):
    ri, ci = _tri_iota(ms[0].shape[0])
    eye = (ri == ci).astype(F32)
    sums = [eye + m for m in ms]
    pows = list(ms)
    for _ in range(int(math.log2(nil)) - 1):
        pows = [_mm(p, p) for p in pows]
        sums = [s + _mm(s, p) for s, p in zip(sums, pows)]
    return sums


def _stack_rows(x, n, width):
    return jnp.concatenate([x[:, h * width:(h + 1) * width] for h in range(n)], axis=0)


def _block_tri(n_blocks, c, reverse):
    ri, ci = _tri_iota(n_blocks * c)
    same = (ri // c) == (ci // c)
    if reverse:
        return jnp.logical_and(same, ri <= ci), jnp.logical_and(same, ri < ci)
    return jnp.logical_and(same, ri >= ci), jnp.logical_and(same, ri > ci)


def _shift_rows(x, n):
    return pltpu.roll(x, n % x.shape[0], 0)


def _conv3(x, prev_row, next_row, w):
    rows = x.shape[0]
    ridx = lax.broadcasted_iota(jnp.int32, x.shape, 0)
    up = jnp.where(ridx == 0, prev_row, _shift_rows(x, 1))
    dn = jnp.where(ridx == rows - 1, next_row, _shift_rows(x, -1))
    return up * w[0:1, :] + x * w[1:2, :] + dn * w[2:3, :]


def _head_sum(x, width):
    n = x.shape[1]
    ri, ci = _tri_iota(n)
    return _mm_exact_rhs(x, (ri // width == ci // width).astype(F32))


def _mod_kernel(c_ref, w_ref, b_ref, o_ref):
    o_ref[...] = _mm3(_silu(c_ref[...]), w_ref[...]) + b_ref[...]


def _modulation(c8, w_mod, b_mod):
    n_layers, d = w_mod.shape[0], w_mod.shape[1]
    return pl.pallas_call(
        _mod_kernel,
        grid=(n_layers, 6),
        in_specs=[
            pl.BlockSpec((8, d), lambda l, j: (0, 0)),
            pl.BlockSpec((None, d, d), lambda l, j: (l, 0, j)),
            pl.BlockSpec((None, None, 1, d), lambda l, j: (l, j, 0, 0)),
        ],
        out_specs=pl.BlockSpec((None, None, 8, d), lambda l, j: (l, j, 0, 0)),
        out_shape=jax.ShapeDtypeStruct((n_layers, 6, 8, d), F32),
        compiler_params=_cparams(2),
        name="adaln_mod",
    )(c8, w_mod, b_mod.reshape(n_layers, 6, 1, d))


def _mod_spec(layer, which, rows_per_tile, seq, n_batch):
    base = (layer * 6 + which) * 8
    tiles_per_batch = seq // rows_per_tile

    def imap(i, *_):
        return (base + jnp.minimum(i // tiles_per_batch, n_batch), 0, 0)

    return pl.BlockSpec((None, 1, D_MODEL), imap)


def _inproj_kernel(x_ref, nw_ref, sh_ref, sc_ref, *refs):
    n = len(refs) // 2
    x = x_ref[...]
    h = x * lax.rsqrt(jnp.mean(x * x, axis=-1, keepdims=True) + NORM_EPS) * nw_ref[...]
    h = (h * (1.0 + sc_ref[...]) + sh_ref[...]).astype(BF16)
    for w_ref, o_ref in zip(refs[:n], refs[n:]):
        o_ref[...] = jnp.dot(h, w_ref[...], preferred_element_type=F32)


def _inproj(x_all, norm_w, mod3, layer, weights, seq, n_batch):
    ntok, d = x_all.shape
    widths = [w.shape[1] for w in weights]
    row = lambda wd: pl.BlockSpec((IN_TM, wd), lambda i: (i, 0))
    return pl.pallas_call(
        _inproj_kernel,
        grid=(ntok // IN_TM,),
        in_specs=[row(d), pl.BlockSpec((1, d), lambda i: (0, 0)),
                  _mod_spec(layer, 0, IN_TM, seq, n_batch), _mod_spec(layer, 1, IN_TM, seq, n_batch)]
                 + [pl.BlockSpec(w.shape, lambda i: (0, 0)) for w in weights],
        out_specs=[row(wd) for wd in widths],
        out_shape=[jax.ShapeDtypeStruct((ntok, wd), F32) for wd in widths],
        compiler_params=_cparams(1),
        name="norm_mod_inproj",
    )(x_all, norm_w.reshape(1, d), mod3, mod3, *weights)


def _chunk_maps(n_lat_rows, seq, ctx_len, chunk):
    lat_chunks, ctx_chunks = seq // chunk, ctx_len // chunk
    ctx_base = n_lat_rows // chunk

    def fwd(b, t):
        return jnp.where(t < ctx_chunks, ctx_base + b * ctx_chunks + t, b * lat_chunks + (t - ctx_chunks))

    def bwd(b, t):
        return jnp.where(t < ctx_chunks, ctx_base + b * ctx_chunks + (ctx_chunks - 1 - t),
                         b * lat_chunks + (lat_chunks - 1 - (t - ctx_chunks)))

    return fwd, bwd, lat_chunks + ctx_chunks


def _ssd_prep_kernel(tiles_per_batch, n_lat_tiles, x_ref, p_ref, n_ref, w_ref, b_ref, o_ref):
    i = pl.program_id(0)
    lat = i < n_lat_tiles
    is_start = jnp.logical_or(jnp.logical_not(lat), i % tiles_per_batch == 0)
    is_end = jnp.logical_or(jnp.logical_not(lat), i % tiles_per_batch == tiles_per_batch - 1)
    prev_row = jnp.where(is_start, 0.0, p_ref[7:8, :])
    next_row = jnp.where(is_end, 0.0, n_ref[0:1, :])
    o_ref[...] = _silu(_conv3(x_ref[...], prev_row, next_row, w_ref[...]) + b_ref[...])


def _ssd_prep(p_m, conv_w, conv_b, seq, n_lat, ctx_len):
    ntok = p_m.shape[0]
    wd = M_CONV_DIM
    assert ctx_len == CONV_TM and seq % CONV_TM == 0
    sub = CONV_TM // 8
    last8 = ntok // 8 - 1
    kern = functools.partial(_ssd_prep_kernel, seq // CONV_TM, n_lat // CONV_TM)
    return pl.pallas_call(
        kern,
        grid=(ntok // CONV_TM,),
        in_specs=[
            pl.BlockSpec((CONV_TM, wd), lambda i: (i, 0)),
            pl.BlockSpec((8, wd), lambda i: (jnp.maximum(i * sub - 1, 0), 0)),
            pl.BlockSpec((8, wd), lambda i: (jnp.minimum((i + 1) * sub, last8), 0)),
            pl.BlockSpec((3, wd), lambda i: (0, 0)),
            pl.BlockSpec((1, wd), lambda i: (0, 0)),
        ],
        out_specs=pl.BlockSpec((CONV_TM, wd), lambda i: (i, 0)),
        out_shape=jax.ShapeDtypeStruct((ntok, wd), F32),
        compiler_params=_cparams(1),
        name="ssd_conv_silu",
    )(p_m, p_m, p_m, conv_w, conv_b.reshape(1, wd))


def _ssd_scan_kernel(xf, bf, cf, sf, xb, bb, cb, sb, par_ref, yf_ref, yb_ref, st_ref):
    q = M_CHUNK

    @pl.when(pl.program_id(1) == 0)
    def _():
        st_ref[...] = jnp.zeros_like(st_ref)

    ri, ci = _tri_iota(q)
    for d, (x_ref, b_ref, c_ref, s_ref, y_ref) in enumerate(((xf, bf, cf, sf, yf_ref), (xb, bb, cb, sb, yb_ref))):
        mask = (ri >= ci) if d == 0 else (ri <= ci)
        last = q - 1 if d == 0 else 0
        dt = _softplus(s_ref[...] + par_ref[0:1, :])
        la = dt * par_ref[1:2, :]
        cum = _mm_exact_lhs(mask.astype(F32), la)
        cum_rows = _to_rows(cum, 8)
        xs, bm, cm = x_ref[...], b_ref[...], c_ref[...]
        gmat = [_mm_nt(cm[:, g * M_STATE:(g + 1) * M_STATE], bm[:, g * M_STATE:(g + 1) * M_STATE])
                for g in range(M_GROUPS)]
        outs = []
        for h in range(M_HEADS):
            c = d * M_HEADS + h
            g = h // (M_HEADS // M_GROUPS)
            cc = cum[:, c:c + 1]
            cr = cum_rows[c:c + 1, :]
            seg = jnp.where(mask, jnp.exp(jnp.minimum(cc - cr, 0.0)), 0.0)
            xin = xs[:, h * M_HEAD_DIM:(h + 1) * M_HEAD_DIM] * dt[:, c:c + 1]
            bm_g = bm[:, g * M_STATE:(g + 1) * M_STATE]
            cm_g = cm[:, g * M_STATE:(g + 1) * M_STATE]
            state = st_ref[d, h]
            outs.append(_mm(gmat[g] * seg, xin) + _mm(cm_g, state) * jnp.exp(cc))
            lc = cum[last:last + 1, c:c + 1]
            st_ref[d, h] = jnp.exp(lc) * state + _mm_tn(bm_g * jnp.exp(lc - cc), xin)
        y_ref[...] = jnp.concatenate(outs, axis=1)


def _ssd_scan(xbc, small, par, n_batch, seq, ctx_len):
    ntok = xbc.shape[0]
    q = M_CHUNK
    fwd, bwd, steps = _chunk_maps(n_batch * seq, seq, ctx_len, q)

    def specs(m):
        return [pl.BlockSpec((q, M_INNER), lambda b, t: (m(b, t), 0)),
                pl.BlockSpec((q, M_INNER), lambda b, t: (m(b, t), 1)),
                pl.BlockSpec((q, M_INNER), lambda b, t: (m(b, t), 2)),
                pl.BlockSpec((q, LANES), lambda b, t: (m(b, t), 0))]

    return pl.pallas_call(
        _ssd_scan_kernel,
        grid=(n_batch, steps),
        in_specs=specs(fwd) + specs(bwd) + [pl.BlockSpec((8, LANES), lambda b, t: (0, 0))],
        out_specs=[pl.BlockSpec((q, M_INNER), lambda b, t: (fwd(b, t), 0)),
                   pl.BlockSpec((q, M_INNER), lambda b, t: (bwd(b, t), 0))],
        out_shape=[jax.ShapeDtypeStruct((ntok, M_INNER), F32)] * 2,
        scratch_shapes=[pltpu.VMEM((2, M_HEADS, M_STATE, M_HEAD_DIM), F32)],
        compiler_params=_cparams(2),
        name="ssd_scan",
    )(xbc, xbc, xbc, small, xbc, xbc, xbc, small, par)


def _dn_norm(x, which):
    if which == 2:
        return x
    outs = []
    for h in range(DN_HEADS):
        xh = x[:, h * DN_HEAD_K:(h + 1) * DN_HEAD_K]
        n = lax.rsqrt(jnp.sum(xh * xh, axis=-1, keepdims=True) + 1e-6)
        if which == 0:
            n = n * (DN_HEAD_K ** -0.5)
        outs.append(xh * n)
    return jnp.concatenate(outs, axis=1)


def _dn_prep_ctx_kernel(q_ref, k_ref, v_ref, w_ref, oq_ref, ok_ref, ov_ref):
    for which, (i_ref, o_ref) in enumerate(((q_ref, oq_ref), (k_ref, ok_ref), (v_ref, ov_ref))):
        x = i_ref[...]
        zero = jnp.zeros((1, x.shape[1]), F32)
        w = w_ref[:, which * DN_KD:(which + 1) * DN_KD]
        o_ref[...] = _dn_norm(_silu(_conv3(x, zero, zero, w)), which)


def _dn_prep_lat_kernel(q_ref, qp_ref, qn_ref, k_ref, kp_ref, kn_ref, v_ref, vp_ref, vn_ref, w_ref,
                        oq_ref, ok_ref, ov_ref):
    g = pl.program_id(1)
    first = g == 0
    last = g == pl.num_programs(1) - 1
    wd = DN_KD
    for which, (i_ref, p_ref, n_ref, o_ref) in enumerate(((q_ref, qp_ref, qn_ref, oq_ref),
                                                         (k_ref, kp_ref, kn_ref, ok_ref),
                                                         (v_ref, vp_ref, vn_ref, ov_ref))):
        rows = i_ref.shape[0]
        cols = [i_ref[:, j, :] for j in range(DN_G)]
        halo_p = jnp.where(first, 0.0, p_ref[7:8, DN_G - 1, :])
        halo_n = jnp.where(last, 0.0, n_ref[0:1, 0, :])
        w = w_ref[:, which * wd:(which + 1) * wd]
        for j in range(DN_G):
            prev_row = halo_p if j == 0 else cols[j - 1][rows - 1:rows, :]
            next_row = halo_n if j == DN_G - 1 else cols[j + 1][0:1, :]
            y = _silu(_conv3(cols[j], prev_row, next_row, w))
            o_ref[:, j * wd:(j + 1) * wd] = _dn_norm(y, which)


def _dn_prep(pq, pk, pv, conv_w, n_batch, seq, ctx_len):
    ntok = pq.shape[0]
    n_lat = n_batch * seq
    wd = DN_KD
    rows = seq // GRID_W
    assert rows == DN_CHUNK and GRID_W % DN_G == 0 and n_lat % ctx_len == 0
    n_ctx = ntok - n_lat
    cin = pl.BlockSpec((ctx_len, wd), lambda b: (n_lat // ctx_len + b, 0))
    cout = pl.BlockSpec((ctx_len, wd), lambda b: (b, 0))
    ctx_out = pl.pallas_call(
        _dn_prep_ctx_kernel,
        grid=(n_batch,),
        in_specs=[cin, cin, cin, pl.BlockSpec((3, 3 * wd), lambda b: (0, 0))],
        out_specs=[cout, cout, cout],
        out_shape=[jax.ShapeDtypeStruct((n_ctx, wd), F32)] * 3,
        compiler_params=_cparams(1),
        name="dn_prep_ctx",
    )(pq, pk, pv, conv_w)
    view = lambda a: a.reshape(ntok // GRID_W, GRID_W, wd)
    n_grp = GRID_W // DN_G
    main = pl.BlockSpec((rows, DN_G, wd), lambda b, g: (b, g, 0))
    prev = pl.BlockSpec((8, DN_G, wd), lambda b, g: (b * (rows // 8) + rows // 8 - 1, jnp.maximum(g - 1, 0), 0))
    nxt = pl.BlockSpec((8, DN_G, wd), lambda b, g: (b * (rows // 8), jnp.minimum(g + 1, n_grp - 1), 0))
    lat_out = pl.pallas_call(
        _dn_prep_lat_kernel,
        grid=(n_batch, n_grp),
        in_specs=[main, prev, nxt] * 3 + [pl.BlockSpec((3, 3 * wd), lambda b, g: (0, 0))],
        out_specs=[pl.BlockSpec((rows, DN_G * wd), lambda b, g: (b, g))] * 3,
        out_shape=[jax.ShapeDtypeStruct((n_lat // GRID_W, GRID_W * wd), F32)] * 3,
        compiler_params=_cparams(2),
        name="dn_prep_lat",
    )(view(pq), view(pq), view(pq), view(pk), view(pk), view(pk), view(pv), view(pv), view(pv), conv_w)
    return lat_out, ctx_out


def _dn_scan_kernel(side_by_side, qf, kf, vf, sf, qb, kb, vb, sb, par_ref, s0_ref, of_ref, ob_ref, s1_ref, st_ref):
    c = DN_CHUNK
    t = pl.program_id(1)

    @pl.when(t == 0)
    def _():
        st_ref[...] = s0_ref[...]

    def sub(ref, s, width):
        return ref[:, s * width:(s + 1) * width] if side_by_side else ref[s * c:(s + 1) * c, :]

    ri, ci = _tri_iota(c)
    dirs = ((qf, kf, vf, sf), (qb, kb, vb, sb))
    chains = [(d, s) for d in range(2) for s in range(SCAN_SUB)]
    masks = [_block_tri(DN_HEADS, c, d == 1) for d in range(2)]
    heads = range(DN_HEADS)
    ops = []
    for d, s in chains:
        q_ref, k_ref, v_ref, s_ref = dirs[d]
        incl_c = (ri >= ci) if d == 0 else (ri <= ci)
        last = c - 1 if d == 0 else 0
        lb, lg = 8 + d * DN_HEADS, 16 + d * DN_HEADS
        sm = sub(s_ref, s, LANES)
        beta_all = _sigmoid(sm)
        gl = _softplus(sm + par_ref[0:1, :]) * par_ref[1:2, :]
        cum = _mm_exact_lhs(incl_c.astype(F32), gl)
        cum_rows = _to_rows(cum, 32)
        beta = jnp.concatenate([beta_all[:, lb + h:lb + h + 1] for h in heads], axis=0)
        cc = jnp.concatenate([cum[:, lg + h:lg + h + 1] for h in heads], axis=0)
        cr = jnp.concatenate([cum_rows[lg + h:lg + h + 1, :] for h in heads], axis=1)
        decay = jnp.where(masks[d][0], jnp.exp(jnp.minimum(cc - cr, 0.0)), 0.0)
        qs = _stack_rows(sub(q_ref, s, DN_KD), DN_HEADS, DN_HEAD_K)
        ks = _stack_rows(sub(k_ref, s, DN_KD), DN_HEADS, DN_HEAD_K)
        vs = _stack_rows(sub(v_ref, s, DN_VD), DN_HEADS, DN_HEAD_V)
        kbeta = ks * beta
        ecc = jnp.exp(cc)
        lcs = [cum[last:last + 1, lg + h:lg + h + 1] for h in heads]
        ops.append(dict(decay=decay, qs=qs, ks=ks, kbeta=kbeta, qe=qs * ecc,
                        rhs=jnp.concatenate([vs * beta, kbeta * ecc], axis=1),
                        kdec=[ks[h * c:(h + 1) * c] * jnp.exp(lcs[h] - cc[h * c:(h + 1) * c]) for h in heads],
                        e_last=[jnp.exp(lc) for lc in lcs]))
    kk = [_mm_nt(o["kbeta"], o["ks"]) for o in ops]
    qk = [_mm_nt(o["qs"], o["ks"]) for o in ops]
    lower = [-jnp.where(masks[d][1], x * o["decay"], 0.0) for (d, _), x, o in zip(chains, kk, ops)]
    attn = [x * o["decay"] for x, o in zip(qk, ops)]
    tinv = _neumann_many(lower, c)
    uw = [_mm(t_, o["rhs"]) for t_, o in zip(tinv, ops)]
    states = [[st_ref[d, h] for h in heads] for d in range(2)]
    for step in range(SCAN_SUB):
        idx = [step, SCAN_SUB + (SCAN_SUB - 1 - step)]
        qw = [[_mm(jnp.concatenate([ops[i]["qe"][h * c:(h + 1) * c], uw[i][h * c:(h + 1) * c, DN_HEAD_V:]], axis=0),
                   states[d][h]) for h in heads] for d, i in enumerate(idx)]
        v_new = [[uw[i][h * c:(h + 1) * c, :DN_HEAD_V] - qw[d][h][c:] for h in heads] for d, i in enumerate(idx)]
        o = [jnp.concatenate([qw[d][h][:c] for h in heads], axis=0)
             + _mm(attn[i], jnp.concatenate(v_new[d], axis=0)) for d, i in enumerate(idx)]
        states = [[ops[i]["e_last"][h] * states[d][h] + _mm_tn(ops[i]["kdec"][h], v_new[d][h]) for h in heads]
                  for d, i in enumerate(idx)]
        for d, (i, o_ref) in enumerate(zip(idx, (of_ref, ob_ref))):
            s = chains[i][1]
            o_cols = jnp.concatenate([o[d][h * c:(h + 1) * c] for h in heads], axis=1)
            if side_by_side:
                o_ref[:, s * DN_VD:(s + 1) * DN_VD] = o_cols
            else:
                o_ref[s * c:(s + 1) * c, :] = o_cols
    for d in range(2):
        for h in heads:
            st_ref[d, h] = states[d][h]

    @pl.when(t == pl.num_programs(1) - 1)
    def _():
        s1_ref[...] = st_ref[...]


def _dn_scan(lat_qkv, ctx_qkv, small, par, n_batch, seq, ctx_len):
    ntok = small.shape[0]
    n_lat = n_batch * seq
    c = DN_CHUNK
    wd = DN_KD
    st_shape = (n_batch, 2, DN_HEADS, DN_HEAD_K, DN_HEAD_V)
    st_spec = pl.BlockSpec((None, 2, DN_HEADS, DN_HEAD_K, DN_HEAD_V), lambda b, t: (b, 0, 0, 0, 0))
    par_spec = pl.BlockSpec((8, LANES), lambda b, t: (0, 0))

    def run(arrs, sm, s0, nsteps, side_by_side, blk_of, sm_blk_of, name):
        fw = lambda f: (lambda b, t: f(b, t))
        bw = lambda f: (lambda b, t: f(b, nsteps - 1 - t))
        shp = lambda w: (c, SCAN_SUB * w) if side_by_side else (SCAN_SUB * c, w)
        ins = lambda o: [pl.BlockSpec(shp(wd), o(blk_of))] * 3 + [pl.BlockSpec(shp(LANES), o(sm_blk_of))]
        return pl.pallas_call(
            functools.partial(_dn_scan_kernel, side_by_side),
            grid=(n_batch, nsteps),
            in_specs=ins(fw) + ins(bw) + [par_spec, st_spec],
            out_specs=[pl.BlockSpec(shp(wd), fw(blk_of)), pl.BlockSpec(shp(wd), bw(blk_of)), st_spec],
            out_shape=[jax.ShapeDtypeStruct(arrs[0].shape, F32)] * 2 + [jax.ShapeDtypeStruct(st_shape, F32)],
            scratch_shapes=[pltpu.VMEM((2, DN_HEADS, DN_HEAD_K, DN_HEAD_V), F32)],
            compiler_params=_cparams(2),
            name=name,
        )(*arrs, sm, *arrs, sm, par, s0)

    ctx_steps = ctx_len // (c * SCAN_SUB)
    ctx_base = n_lat // (c * SCAN_SUB)
    ocf, ocb, s_ctx = run(ctx_qkv, small, jnp.zeros(st_shape, F32), ctx_steps, False,
                          lambda b, j: (b * ctx_steps + j, 0),
                          lambda b, j: (ctx_base + b * ctx_steps + j, 0), "dn_scan_ctx")
    olf, olb, _ = run(lat_qkv, small.reshape(ntok // GRID_W, GRID_W * LANES), s_ctx, GRID_W // SCAN_SUB, True,
                      lambda b, j: (b, j), lambda b, j: (b, j), "dn_scan_lat")
    return olf.reshape(n_lat, wd), olb.reshape(n_lat, wd), ocf, ocb


def _rw_post(x, shifted, mu_ref, vec_ref, a2_ref, g2_ref, w2_ref, o_ref):
    pm = x + (shifted - x) * mu_ref[...]
    r = pm[:, 0:RW_DIM]
    k = pm[:, RW_DIM:2 * RW_DIM]
    v = pm[:, 2 * RW_DIM:3 * RW_DIM]
    base = 3 * RW_DIM
    ad = pm[:, base + 2 * RW_DECAY_LORA:base + 2 * RW_DECAY_LORA + RW_ICLR_LORA]
    gd = pm[:, base + 2 * RW_DECAY_LORA + RW_ICLR_LORA:RW_COLS]
    a0, k_k, k_a, r_k = vec_ref[0:1, :], vec_ref[1:2, :], vec_ref[2:3, :], vec_ref[3:4, :]
    a = _sigmoid(a0 + _mm(ad, a2_ref[...]))
    g = _mm(_sigmoid(gd), g2_ref[...])
    kx = k * k_k
    kk = kx * lax.rsqrt(_head_sum(kx * kx, RW_HEAD) + 1e-6)
    k2 = k * (1.0 + (a - 1.0) * k_a)
    bonus = _head_sum(r * k2 * r_k, RW_HEAD) * v
    cols = [r, k2, v, -kk, kk * a]
    for d in range(2):
        wd_d = pm[:, base + d * RW_DECAY_LORA:base + (d + 1) * RW_DECAY_LORA]
        pre = vec_ref[4 + d:5 + d, :] + _mm(jnp.tanh(wd_d), w2_ref[d])
        cols.append(-jnp.exp(-_softplus(-pre) - 0.5))
    cols += [bonus, g]
    for j, cval in enumerate(cols):
        o_ref[:, j * RW_DIM:(j + 1) * RW_DIM] = cval


def _rw_prep_kernel(tiles_per_batch, n_lat_tiles, x_ref, u_ref, d_ref, mu_ref, vec_ref, a2_ref, g2_ref, w2_ref, o_ref):
    i = pl.program_id(0)
    x = x_ref[...]
    rows = x.shape[0]
    ridx = lax.broadcasted_iota(jnp.int32, x.shape, 0)
    lane = lax.broadcasted_iota(jnp.int32, x.shape, 1)
    before = _shift_rows(x, 1)
    after = _shift_rows(x, -1)
    col = ridx % GRID_W
    left = jnp.where(col == 0, 0.0, before)
    right = jnp.where(col == GRID_W - 1, 0.0, after)
    top = jnp.where(i % tiles_per_batch == 0, 0.0, u_ref[...])
    bot = jnp.where(i % tiles_per_batch == tiles_per_batch - 1, 0.0, d_ref[...])
    up = jnp.concatenate([top, x[:rows - GRID_W, :]], axis=0)
    down = jnp.concatenate([x[GRID_W:, :], bot], axis=0)
    sel = lane % 4
    shifted_lat = jnp.where(sel == 0, left, jnp.where(sel == 1, right, jnp.where(sel == 2, up, down)))
    prev = jnp.where(ridx == 0, 0.0, before)
    nxt = jnp.where(ridx == rows - 1, 0.0, after)
    shifted_ctx = jnp.where(lane % 2 == 0, prev, nxt)
    shifted = jnp.where(i < n_lat_tiles, shifted_lat, shifted_ctx)
    _rw_post(x, shifted, mu_ref, vec_ref, a2_ref, g2_ref, w2_ref, o_ref)


def _rw_prep(p_rw, mu, vec, a2, g2, w2, n_batch, seq, ctx_len):
    ntok, wd = p_rw.shape
    n_lat = n_batch * seq
    wo = RW_NOUT * RW_DIM
    tm = RW_TM
    assert ctx_len == tm and seq % tm == 0 and tm % GRID_W == 0
    per = tm // GRID_W
    last_blk = ntok // GRID_W - 1
    consts = [pl.BlockSpec(a.shape, (lambda i, nd=a.ndim: (0,) * nd)) for a in (mu, vec, a2, g2, w2)]
    return pl.pallas_call(
        functools.partial(_rw_prep_kernel, seq // tm, n_lat // tm),
        grid=(ntok // tm,),
        in_specs=[pl.BlockSpec((tm, wd), lambda i: (i, 0)),
                  pl.BlockSpec((GRID_W, wd), lambda i: (jnp.maximum(i * per - 1, 0), 0)),
                  pl.BlockSpec((GRID_W, wd), lambda i: (jnp.minimum((i + 1) * per, last_blk), 0))] + consts,
        out_specs=pl.BlockSpec((tm, wo), lambda i: (i, 0)),
        out_shape=jax.ShapeDtypeStruct((ntok, wo), F32),
        compiler_params=_cparams(1),
        name="rw_prep",
    )(p_rw, p_rw, p_rw, mu, vec, a2, g2, w2)


def _rw_scan_kernel(rf, kf, vf, af, bf, wf, rb, kb, vb, ab, bb, wb, yf_ref, yb_ref, st_ref):
    c = RW_CHUNK

    @pl.when(pl.program_id(1) == 0)
    def _():
        st_ref[...] = jnp.zeros_like(st_ref)

    ri, ci = _tri_iota(c)
    nr = RW_HEADS * c
    head_mask = (lax.broadcasted_iota(jnp.int32, (nr, RW_DIM), 0) // c
                 == lax.broadcasted_iota(jnp.int32, (nr, RW_DIM), 1) // RW_HEAD)

    def stacked(x):
        xb = x.astype(BF16)
        return jnp.where(head_mask, jnp.concatenate([xb] * RW_HEADS, axis=0), jnp.zeros((), BF16))

    dirs = ((rf, kf, vf, af, bf, wf), (rb, kb, vb, ab, bb, wb))
    chains = [(d, s) for d in range(2) for s in range(SCAN_SUB)]
    masks = [_block_tri(RW_HEADS, c, d == 1) for d in range(2)]
    ops = []
    for d, s in chains:
        incl_c = (ri >= ci) if d == 0 else (ri <= ci)
        last = c - 1 if d == 0 else 0
        r, k, v, a, b, lw = [ref[s * c:(s + 1) * c, :] for ref in dirs[d]]
        cum = _mm_exact_lhs(incl_c.astype(F32), lw)
        lc = cum[last:last + 1, :]
        e_neg = jnp.exp(-cum)
        e_rel = jnp.exp(lc - cum)
        ops.append(dict(rt=stacked(r * jnp.exp(cum)), at=stacked(a * jnp.exp(cum - lw)),
                        bt=stacked(b * e_neg), kt=stacked(k * e_neg), bl=stacked(b * e_rel), kl=stacked(k * e_rel),
                        vs=stacked(v), e_last=jnp.exp(lc)))
    nr2 = 2 * nr
    bk = [jnp.concatenate([o["bt"], o["kt"]], axis=0) for o in ops]
    ar = [jnp.concatenate([o["at"], o["rt"]], axis=0) for o in ops]
    blkl = [jnp.concatenate([o["bl"], o["kl"]], axis=0) for o in ops]
    raw = [_mm_nt(x, y) for x, y in zip(ar, bk)]
    aab = [jnp.where(masks[d][1], x[:nr, :nr], 0.0) for (d, _), x in zip(chains, raw)]
    aak = [jnp.where(masks[d][1], x[:nr, nr:], 0.0) for (d, _), x in zip(chains, raw)]
    arb = [jnp.where(masks[d][0], x[nr:, :nr], 0.0) for (d, _), x in zip(chains, raw)]
    ark = [jnp.where(masks[d][0], x[nr:, nr:], 0.0) for (d, _), x in zip(chains, raw)]
    akv = [_mm(jnp.concatenate([m1, m2], axis=0), o["vs"]) for m1, m2, o in zip(aak, ark, ops)]
    aak_v = [x[:nr] for x in akv]
    ark_v = [x[nr:] for x in akv]
    tinv = _neumann_many(aab, c)
    state = [st_ref[0], st_ref[1]]
    for step in range(SCAN_SUB):
        idx = [step, SCAN_SUB + (SCAN_SUB - 1 - step)]
        ars = [_mm_nt(ar[i], state[d]) for d, i in enumerate(idx)]
        u = [_mm(tinv[i], ars[d][:nr] + aak_v[i]) for d, i in enumerate(idx)]
        y = [ars[d][nr:] + _mm(arb[i], u[d]) + ark_v[i] for d, i in enumerate(idx)]
        state = [state[d] * ops[i]["e_last"]
                 + _mm_tn(jnp.concatenate([u[d].astype(BF16), ops[i]["vs"]], axis=0), blkl[i])
                 for d, i in enumerate(idx)]
        assert ars[0].shape[0] == nr2
        for d, (i, y_ref) in enumerate(zip(idx, (yf_ref, yb_ref))):
            s = chains[i][1]
            y_ref[s * c:(s + 1) * c, :] = functools.reduce(
                lambda p, q: p + q, [y[d][h * c:(h + 1) * c] for h in range(RW_HEADS)])
    st_ref[0] = state[0]
    st_ref[1] = state[1]


def _rw_scan(rwp, n_batch, seq, ctx_len):
    ntok = rwp.shape[0]
    c = RW_CHUNK * SCAN_SUB
    fwd, bwd, steps = _chunk_maps(n_batch * seq, seq, ctx_len, c)

    def specs(m, wcol):
        return [pl.BlockSpec((c, RW_DIM), lambda b, t, j=j: (m(b, t), j)) for j in (0, 1, 2, 3, 4, wcol)]

    return pl.pallas_call(
        _rw_scan_kernel,
        grid=(n_batch, steps),
        in_specs=specs(fwd, 5) + specs(bwd, 6),
        out_specs=[pl.BlockSpec((c, RW_DIM), lambda b, t: (fwd(b, t), 0)),
                   pl.BlockSpec((c, RW_DIM), lambda b, t: (bwd(b, t), 0))],
        out_shape=[jax.ShapeDtypeStruct((ntok, RW_DIM), F32)] * 2,
        scratch_shapes=[pltpu.VMEM((2, RW_DIM, RW_DIM), F32)],
        compiler_params=_cparams(2),
        name="rw_scan",
    )(*([rwp] * 12))


def _outproj_kernel(n_lat_tiles, x_ref, ymf, ymb, xs_ref, z_ref, odf, odb, ocf, ocb, gate_ref, yrf, yrb, bonus_ref,
                    g_ref, vec_ref, dnw_ref, wout_ref, gmsa_ref, n2w_ref, sh_ref, sc_ref, wr_ref, br_ref,
                    xo_ref, h2_ref, route_ref, cnt_ref, run_ref):
    tm = x_ref.shape[0]

    @pl.when(pl.program_id(0) == 0)
    def _():
        run_ref[...] = jnp.zeros_like(run_ref)

    ym = (xs_ref[...] * vec_ref[0:1, :] + ymf[...] + ymb[...]) * _silu(z_ref[...])
    gw = M_INNER // M_GROUPS
    parts = []
    for g in range(M_GROUPS):
        yg = ym[:, g * gw:(g + 1) * gw]
        parts.append(yg * lax.rsqrt(jnp.mean(yg * yg, axis=-1, keepdims=True) + 1e-5))
    mix_m = jnp.concatenate(parts, axis=1) * vec_ref[1:2, :]
    od = jnp.where(pl.program_id(0) < n_lat_tiles, odf[...] + odb[...], ocf[...] + ocb[...])
    parts = []
    for h in range(DN_HEADS):
        oh = od[:, h * DN_HEAD_V:(h + 1) * DN_HEAD_V]
        parts.append(oh * lax.rsqrt(jnp.mean(oh * oh, axis=-1, keepdims=True) + 1e-6))
    mix_d = jnp.concatenate(parts, axis=1) * dnw_ref[...] * _silu(gate_ref[...])
    yr = yrf[...] + yrb[...]
    mean = _head_sum(yr, RW_HEAD) * (1.0 / RW_HEAD)
    cen = yr - mean
    var = _head_sum(cen * cen, RW_HEAD) * (1.0 / RW_HEAD)
    mix_r = ((cen * lax.rsqrt(var + RW_LN_EPS)) * vec_ref[2:3, :] + vec_ref[3:4, :] + bonus_ref[...]) * g_ref[...]

    attn = (_mm(mix_m, wout_ref[0:M_INNER, :]) + _mm(mix_d, wout_ref[M_INNER:M_INNER + DN_VD, :])
            + _mm(mix_r, wout_ref[M_INNER + DN_VD:, :]))
    xn = x_ref[...] + gmsa_ref[...] * attn
    xo_ref[...] = xn
    h2 = xn * lax.rsqrt(jnp.mean(xn * xn, axis=-1, keepdims=True) + NORM_EPS) * n2w_ref[...]
    h2 = h2 * (1.0 + sc_ref[...]) + sh_ref[...]
    h2_ref[...] = h2

    logits = _mm3(h2, wr_ref[...]) + br_ref[...]
    lane = lax.broadcasted_iota(jnp.int32, (tm, LANES), 1)
    neg = jnp.float32(-jnp.inf)
    lg = jnp.where(lane < N_EXPERTS, logits, neg)
    tops, idxs = [], []
    onehot = jnp.zeros((tm, LANES), F32)
    for _ in range(TOP_K):
        m = jnp.max(lg, axis=-1, keepdims=True)
        idx = jnp.min(jnp.where(lg == m, lane, LANES), axis=-1, keepdims=True)
        sel = lane == idx
        tops.append(m)
        idxs.append(idx)
        lg = jnp.where(sel, neg, lg)
        onehot = onehot + sel.astype(F32)
    ri, ci = _tri_iota(tm)
    before = jnp.dot((ri > ci).astype(BF16), onehot.astype(BF16), preferred_element_type=F32) + run_ref[0:1, :]
    run_ref[0:1, :] = run_ref[0:1, :] + jnp.sum(onehot, axis=0, keepdims=True)
    exps = [jnp.exp(m - tops[0]) for m in tops]
    denom = exps[0] + exps[1] + exps[2] + exps[3]
    route = jnp.zeros((tm, LANES), F32)
    for k in range(TOP_K):
        rank = jnp.sum(jnp.where(lane == idxs[k], before, 0.0), axis=-1, keepdims=True)
        route = jnp.where(lane == k, idxs[k].astype(F32), route)
        route = jnp.where(lane == TOP_K + k, exps[k] / denom, route)
        route = jnp.where(lane == 2 * TOP_K + k, rank, route)
    route_ref[...] = route
    cnt_ref[...] = run_ref[...]


def _outproj(x_all, ssd_y, xbc, p_m, dn_o, p_gate, rw_y, rwp, vec, dnw, w_out, mod3, layer, norm2_w,
             w_router, b_router, n_rows, seq, n_batch):
    d = x_all.shape[1]
    tm = OUT_TM
    row = lambda wd, col=0: pl.BlockSpec((tm, wd), lambda i, col=col: (i, col))
    const = lambda a: pl.BlockSpec(a.shape, lambda i: (0,) * a.ndim)
    n2 = norm2_w.reshape(1, d)
    n_lat_tiles = n_batch * seq // tm
    lat_row = pl.BlockSpec((tm, DN_VD), lambda i: (jnp.minimum(i, n_lat_tiles - 1), 0))
    ctx_row = pl.BlockSpec((tm, DN_VD), lambda i: (jnp.maximum(i - n_lat_tiles, 0), 0))
    ins = [x_all, ssd_y[0], ssd_y[1], xbc, p_m, dn_o[0], dn_o[1], dn_o[2], dn_o[3], p_gate, rw_y[0], rw_y[1],
           rwp, rwp, vec, dnw, w_out, mod3, n2, mod3, mod3, w_router, b_router]
    specs = [row(d), row(M_INNER), row(M_INNER), row(M_INNER, 0), row(M_INNER, M_CONV_DIM // M_INNER),
             lat_row, lat_row, ctx_row, ctx_row, row(DN_VD), row(RW_DIM), row(RW_DIM), row(RW_DIM, 7),
             row(RW_DIM, 8), const(vec), const(dnw), const(w_out), _mod_spec(layer, 2, tm, seq, n_batch),
             const(n2), _mod_spec(layer, 3, tm, seq, n_batch), _mod_spec(layer, 4, tm, seq, n_batch),
             const(w_router), const(b_router)]
    return pl.pallas_call(
        functools.partial(_outproj_kernel, n_lat_tiles),
        grid=(n_rows // tm,),
        in_specs=specs,
        out_specs=[row(d), row(d), row(LANES), pl.BlockSpec((8, LANES), lambda i: (0, 0))],
        out_shape=[jax.ShapeDtypeStruct((n_rows, d), F32), jax.ShapeDtypeStruct((n_rows, d), F32),
                   jax.ShapeDtypeStruct((n_rows, LANES), F32), jax.ShapeDtypeStruct((8, LANES), F32)],
        scratch_shapes=[pltpu.VMEM((8, LANES), F32)],
        compiler_params=_cparams(1),
        name="finish_outproj_router",
    )(*ins)


def _dispatch_kernel(dest_ref, h_ref, zeros_ref, xs_ref, sem):
    del zeros_ref
    tm = h_ref.shape[0]

    def copy(i, k):
        return pltpu.make_async_copy(h_ref.at[pl.ds(i, 1)], xs_ref.at[pl.ds(dest_ref[0, i * TOP_K + k], 1)], sem)

    def start(i, carry):
        for k in range(TOP_K):
            copy(i, k).start(priority=k % 2)
        return carry

    def wait(i, carry):
        for k in range(TOP_K):
            copy(i, k).wait()
        return carry

    lax.fori_loop(0, tm, start, 0)
    lax.fori_loop(0, tm, wait, 0)


def _dispatch(h2, dest, n_slots):
    n, d = h2.shape
    tm = MOE_TM
    return pl.pallas_call(
        _dispatch_kernel,
        grid=(n // tm,),
        in_specs=[pl.BlockSpec((None, 1, tm * TOP_K), lambda i: (i, 0, 0), memory_space=pltpu.SMEM),
                  pl.BlockSpec((tm, d), lambda i: (i, 0)),
                  pl.BlockSpec(memory_space=pl.ANY)],
        out_specs=pl.BlockSpec(memory_space=pl.ANY),
        out_shape=jax.ShapeDtypeStruct((n_slots, d), F32),
        scratch_shapes=[pltpu.SemaphoreType.DMA(())],
        input_output_aliases={2: 0},
        compiler_params=_cparams(1),
        name="moe_dispatch",
    )(dest.reshape(n // tm, 1, tm * TOP_K), h2, jnp.zeros((n_slots, d), F32))


def _moe_kernel(be_ref, nv_ref, x_ref, wgu_ref, bgu_ref, wdn_ref, bdn_ref, y_ref, wgu_bf, wdn_bf):
    j = pl.program_id(0)

    @pl.when(j < nv_ref[0])
    def _():
        changed = jnp.logical_or(j == 0, be_ref[j] != be_ref[jnp.maximum(j - 1, 0)])

        @pl.when(changed)
        def _():
            wgu_bf[...] = wgu_ref[...].astype(BF16)
            wdn_bf[...] = wdn_ref[...].astype(BF16)

        gu = jnp.dot(x_ref[...].astype(BF16), wgu_bf[...], preferred_element_type=F32) + bgu_ref[...]
        g_ = jnp.minimum(gu[:, :D_FF], SWIGLU_LIMIT)
        u_ = jnp.clip(gu[:, D_FF:], -SWIGLU_LIMIT, SWIGLU_LIMIT)
        act = g_ * _sigmoid(SWIGLU_ALPHA * g_) * (u_ + 1.0)
        y_ref[...] = jnp.dot(act.astype(BF16), wdn_bf[...], preferred_element_type=F32) + bdn_ref[...]

    @pl.when(j >= nv_ref[0])
    def _():
        y_ref[...] = jnp.zeros_like(y_ref)


def _moe(xs, block_expert, n_valid, layer, w_gate_up, b_gate_up, w_down, b_down):
    n_slots, d = xs.shape
    bm = MOE_BLOCK
    n_layers, n_exp, _, f2 = w_gate_up.shape
    blk = lambda j, be, nv: (jnp.minimum(j, nv[0] - 1), 0)
    grid_spec = pltpu.PrefetchScalarGridSpec(
        num_scalar_prefetch=2,
        grid=(n_slots // bm,),
        in_specs=[pl.BlockSpec((bm, d), blk),
                  pl.BlockSpec((None, None, d, f2), lambda j, be, nv: (layer, be[j], 0, 0)),
                  pl.BlockSpec((None, None, 1, f2), lambda j, be, nv: (layer, be[j], 0, 0)),
                  pl.BlockSpec((None, None, f2 // 2, d), lambda j, be, nv: (layer, be[j], 0, 0)),
                  pl.BlockSpec((None, None, 1, d), lambda j, be, nv: (layer, be[j], 0, 0))],
        out_specs=pl.BlockSpec((bm, d), lambda j, be, nv: (j, 0)),
        scratch_shapes=[pltpu.VMEM((d, f2), BF16), pltpu.VMEM((f2 // 2, d), BF16)],
    )
    return pl.pallas_call(
        _moe_kernel,
        grid_spec=grid_spec,
        out_shape=jax.ShapeDtypeStruct((n_slots, d), F32),
        compiler_params=_cparams(1),
        name="moe_experts",
    )(block_expert, n_valid, xs, w_gate_up, b_gate_up.reshape(n_layers, n_exp, 1, f2), w_down,
      b_down.reshape(n_layers, n_exp, 1, d))


def _combine_kernel(final_norm, dest_ref, x_ref, route_ref, g_ref, nf_ref, y_ref, o_ref, buf, sem):
    tm = x_ref.shape[0]

    def copy(i, k):
        return pltpu.make_async_copy(y_ref.at[pl.ds(dest_ref[0, i * TOP_K + k], 1)], buf.at[k, pl.ds(i, 1)], sem)

    def start(i, carry):
        for k in range(TOP_K):
            copy(i, k).start(priority=k % 2)
        return carry

    def wait(i, carry):
        for k in range(TOP_K):
            copy(i, k).wait()
        return carry

    lax.fori_loop(0, tm, start, 0)
    lax.fori_loop(0, tm, wait, 0)
    route = route_ref[...]
    acc = jnp.zeros(x_ref.shape, F32)
    for k in range(TOP_K):
        acc = acc + route[:, TOP_K + k:TOP_K + k + 1] * buf[k]
    out = x_ref[...] + g_ref[...] * acc
    if final_norm:
        out = out * lax.rsqrt(jnp.mean(out * out, axis=-1, keepdims=True) + NORM_EPS) * nf_ref[...]
    o_ref[...] = out


def _combine(x_new, route, dest, y_slots, mod3, layer, norm_f_w, final_norm, seq, n_batch):
    n, d = x_new.shape
    tm = MOE_TM
    row = lambda wd: pl.BlockSpec((tm, wd), lambda i: (i, 0))
    return pl.pallas_call(
        functools.partial(_combine_kernel, final_norm),
        grid=(n // tm,),
        in_specs=[pl.BlockSpec((None, 1, tm * TOP_K), lambda i: (i, 0, 0), memory_space=pltpu.SMEM),
                  row(d), row(LANES), _mod_spec(layer, 5, tm, seq, n_batch),
                  pl.BlockSpec((1, d), lambda i: (0, 0)),
                  pl.BlockSpec(memory_space=pl.ANY)],
        out_specs=row(d),
        out_shape=jax.ShapeDtypeStruct((n, d), F32),
        scratch_shapes=[pltpu.VMEM((TOP_K, tm, d), F32), pltpu.SemaphoreType.DMA(())],
        compiler_params=_cparams(1),
        name="moe_combine",
    )(dest.reshape(n // tm, 1, tm * TOP_K), x_new, route, mod3, norm_f_w.reshape(1, d), y_slots)


def _routing_tables(route, counts, n_slots):
    bm = MOE_BLOCK
    top_e = route[:, :TOP_K].astype(jnp.int32)
    rank = route[:, 2 * TOP_K:3 * TOP_K].astype(jnp.int32)
    cnt = counts[0, :N_EXPERTS].astype(jnp.int32)
    padded = (cnt + bm - 1) // bm * bm
    ends = jnp.cumsum(padded)
    dest = (ends - padded)[top_e] + rank
    n_blocks = n_slots // bm
    n_valid = ends[-1] // bm
    blk = jnp.minimum(jnp.arange(n_blocks, dtype=jnp.int32), n_valid - 1)
    block_expert = jnp.sum((blk[:, None] >= (ends // bm)[None, :]).astype(jnp.int32), axis=1)
    block_expert = jnp.minimum(block_expert, N_EXPERTS - 1)
    return dest.reshape(-1), block_expert, n_valid.reshape(1).astype(jnp.int32)


def _pad_cols(a, width):
    return jnp.pad(a, [(0, 0)] * (a.ndim - 1) + [(0, width - a.shape[-1])])


def kernel(x, c, ctx, c_ctx, w_mod, b_mod, norm1_w, norm2_w, w_in, w_out, m_conv_w, m_conv_b, m_dt_bias, m_a_log, m_d, m_norm_w, dn_conv_w, dn_dt_bias, dn_a_log, dn_norm_w, rw_mu, rw_w0, rw_w2, rw_a0, rw_a2, rw_g2, rw_k_k, rw_k_a, rw_r_k, rw_ln_w, rw_ln_b, w_router, b_router, w_gate_up, b_gate_up, w_down, b_down, norm_f_w):
    n_batch, seq, d = x.shape
    ctx_len = ctx.shape[1]
    depth = w_mod.shape[0]
    n_lat, n_ctx = n_batch * seq, n_batch * ctx_len
    assert d == D_MODEL and seq // GRID_W == DN_CHUNK and n_batch + 1 <= 8

    x_all = jnp.concatenate([x.reshape(n_lat, d), ctx.reshape(n_ctx, d)], axis=0)
    c8 = jnp.zeros((8, d), F32).at[:n_batch].set(c).at[n_batch].set(c_ctx)
    mod3 = _modulation(c8, w_mod, b_mod).reshape(depth * 6 * 8, 1, d)

    out = None
    for i in range(depth):
        last = i == depth - 1
        wi = w_in[i]
        mo, do, ro = 0, M_COLS, M_COLS + DN_COLS
        w_m = jnp.concatenate([wi[:, mo + M_INNER:mo + M_INNER + M_CONV_DIM], wi[:, mo:mo + M_INNER]], axis=1)
        w_q = wi[:, do:do + DN_KD]
        w_k = wi[:, do + DN_KD:do + 2 * DN_KD]
        w_v = wi[:, do + 2 * DN_KD:do + DN_CONV_DIM]
        w_g = wi[:, do + DN_CONV_DIM:do + DN_CONV_DIM + DN_VD]
        w_r = _pad_cols(wi[:, ro:ro + RW_COLS], RW_COLS_PAD)
        w_s = _pad_cols(jnp.concatenate([wi[:, mo + M_INNER + M_CONV_DIM:mo + M_COLS],
                                         wi[:, do + DN_CONV_DIM + DN_VD:do + DN_COLS]], axis=1), LANES)
        weights = [w.astype(BF16) for w in (w_m, w_q, w_k, w_v, w_g, w_r, w_s)]
        p_m, p_q, p_k, p_v, p_gate, p_rw, small = _inproj(x_all, norm1_w[i], mod3, i, weights, seq, n_batch)

        xbc = _ssd_prep(p_m, m_conv_w[i], m_conv_b[i], seq, n_lat, ctx_len)
        ssd_par = jnp.zeros((8, LANES), F32)
        ssd_par = ssd_par.at[0, :2 * M_HEADS].set(m_dt_bias[i].reshape(-1))
        ssd_par = ssd_par.at[1, :2 * M_HEADS].set(-jnp.exp(m_a_log[i].astype(F32)).reshape(-1))
        ssd_y = _ssd_scan(xbc, small, ssd_par, n_batch, seq, ctx_len)

        lat_qkv, ctx_qkv = _dn_prep(p_q, p_k, p_v, dn_conv_w[i], n_batch, seq, ctx_len)
        dn_par = jnp.zeros((8, LANES), F32)
        dn_par = dn_par.at[0, 16:16 + 2 * DN_HEADS].set(dn_dt_bias[i].reshape(-1))
        dn_par = dn_par.at[1, 16:16 + 2 * DN_HEADS].set(-jnp.exp(dn_a_log[i].astype(F32)).reshape(-1))
        dn_o = _dn_scan(lat_qkv, ctx_qkv, small, dn_par, n_batch, seq, ctx_len)

        rw_vec = jnp.stack([rw_a0[i], rw_k_k[i], rw_k_a[i], rw_r_k[i].reshape(-1), rw_w0[i, 0], rw_w0[i, 1],
                            jnp.zeros((RW_DIM,), F32), jnp.zeros((RW_DIM,), F32)])
        rwp = _rw_prep(p_rw, _pad_cols(rw_mu[i], RW_COLS_PAD).reshape(1, -1), rw_vec, rw_a2[i], rw_g2[i], rw_w2[i],
                       n_batch, seq, ctx_len)
        rw_y = _rw_scan(rwp, n_batch, seq, ctx_len)

        n_rows = n_lat if last else n_lat + n_ctx
        fin_vec = jnp.stack([jnp.repeat(m_d[i], M_HEAD_DIM), m_norm_w[i], rw_ln_w[i], rw_ln_b[i]]
                            + [jnp.zeros((RW_DIM,), F32)] * 4)
        dnw = jnp.tile(dn_norm_w[i], DN_HEADS).reshape(1, DN_VD)
        x_new, h2, route, counts = _outproj(
            x_all, ssd_y, xbc, p_m, dn_o, p_gate, rw_y, rwp, fin_vec, dnw, w_out[i].astype(BF16), mod3, i,
            norm2_w[i], _pad_cols(w_router[i], LANES), _pad_cols(b_router[i].reshape(1, -1), LANES),
            n_rows, seq, n_batch)

        n_slots = (n_rows * TOP_K // MOE_BLOCK + N_EXPERTS) * MOE_BLOCK
        dest, block_expert, n_valid = _routing_tables(route, counts, n_slots)
        xs = _dispatch(h2, dest, n_slots)
        y_slots = _moe(xs, block_expert, n_valid, i, w_gate_up, b_gate_up, w_down, b_down)
        res = _combine(x_new, route, dest, y_slots, mod3, i, norm_f_w, last, seq, n_batch)
        if last:
            out = res
        else:
            x_all = res
    return out.reshape(n_batch, seq, d)
```
